```python
import jax, jax.numpy as jnp
from jax import lax
import numpy as np

D_MODEL = 1024
BATCH = 4
SEQ = 4096
DEPTH = 4
DEC_BATCH = 128
DEC_SEQ = 4
PAST_LEN = 2048
PAGE_SIZE = 128

BRANCH_W = D_MODEL // 2
N_BRANCH = 4
SB_HEADS = 8
SB_HEAD_DIM = BRANCH_W // SB_HEADS
SB_BLOCK = 128
SB_BIAS_INIT = -7.0
CONV_W = 3
POOL_WINDOWS = (2, 4, 8, 16)
POOL_GROUP = BRANCH_W // len(POOL_WINDOWS)
POOL_PAD = max(POOL_WINDOWS) - 1
SG_CHUNK = 128
SG_GROUPS = 8
SG_GROUP_W = BRANCH_W // SG_GROUPS
D_FF = 4 * D_MODEL
N_MOD = 6
D_IN = 9 * BRANCH_W + N_BRANCH * D_MODEL
NORM_EPS = 1e-6

kernel_name = 'hybrid_stickbreak_conv_pool_sgmlp_decoder_step'


def rmsnorm(x, g):
    x32 = x.astype(jnp.float32)
    y = x32 * lax.rsqrt(jnp.mean(x32 * x32, axis=-1, keepdims=True) + NORM_EPS)
    return (y * g.astype(jnp.float32)).astype(x.dtype)


def sb_attend(q, k, v, bias, offset):
    tq, tk = q.shape[1], k.shape[1]
    z = jnp.einsum('bthd,bshd->bhts', q, k).astype(jnp.float32) * (q.shape[-1] ** -0.5)
    z = z + bias.astype(jnp.float32)[None, :, None, None]
    mask = jnp.arange(tk)[None, :] < (offset + jnp.arange(tq))[:, None]
    log_beta = jax.nn.log_sigmoid(z)
    log_rest = jnp.where(mask, jax.nn.log_sigmoid(-z), 0.0)
    later = lax.cumsum(log_rest, axis=3, reverse=True) - log_rest
    w = jnp.where(mask, jnp.exp(log_beta + later), 0.0)
    return jnp.einsum('bhts,bshd->bthd', w.astype(v.dtype), v)


def sb_prompt(q, k, v, bias):
    T = q.shape[1]
    outs = [sb_attend(q[:, s:s + SB_BLOCK], k[:, :s + SB_BLOCK], v[:, :s + SB_BLOCK], bias, s)
            for s in range(0, T, SB_BLOCK)]
    return jnp.concatenate(outs, axis=1)


def gather_pages(cache_l, page_table):
    pages = cache_l[page_table]
    b, n, p = pages.shape[:3]
    return pages.reshape(b, n * p, SB_HEADS, SB_HEAD_DIM)


def short_conv(xb, prev, w):
    T = xb.shape[1]
    xp = jnp.concatenate([prev.astype(xb.dtype), xb], axis=1)
    y = sum(w[i] * xp[:, i:i + T] for i in range(CONV_W))
    return y, xp[:, -(CONV_W - 1):]


def pool_mix(xc, prev, pos0, w_pool, scale):
    T = xc.shape[1]
    xp = jnp.concatenate([prev.astype(xc.dtype), xc], axis=1)
    cs = jnp.cumsum(xp.astype(jnp.float32), axis=1)
    cs = jnp.concatenate([jnp.zeros_like(cs[:, :1]), cs], axis=1)
    pos = pos0 + jnp.arange(T)
    outs = []
    for g, win in enumerate(POOL_WINDOWS):
        ch = slice(g * POOL_GROUP, (g + 1) * POOL_GROUP)
        wsum = cs[:, POOL_PAD + 1:POOL_PAD + 1 + T, ch] - cs[:, POOL_PAD + 1 - win:POOL_PAD + 1 - win + T, ch]
        cnt = jnp.minimum(pos + 1, win).astype(jnp.float32)[None, :, None]
        diff = (wsum / cnt).astype(xc.dtype) - xc[..., ch]
        outs.append(diff @ w_pool[g])
    y = jnp.concatenate(outs, axis=-1) * scale
    return y, xp[:, -POOL_PAD:]


def spatial_gate(u, v, w_s, b_s, g_v):
    B, T, C = v.shape
    L = min(T, SG_CHUNK)
    vn = rmsnorm(v, g_v)
    vr = vn.reshape(B, T // L, L, SG_GROUPS, SG_GROUP_W)
    w = jnp.tril(w_s[:, :L, :L])
    mixed = jnp.einsum('gts,bnsgc->bntgc', w, vr) + b_s[:, :L].T[None, None, :, :, None]
    return u * mixed.reshape(B, T, C), vn


def trunk_layer(x, c, past, pos0, w_ada, b_ada, norm_mix, norm_ffn, w_in, sb_bias, conv_w, pool_w,
                pool_scale, sg_w, sg_b, sg_norm, w_branch, w_o, w_ff1, w_ff2):
    B, T, _ = x.shape
    mod = jax.nn.silu(c) @ w_ada + b_ada
    sh1, sc1, g1, sh2, sc2, g2 = jnp.split(mod[:, None, :], N_MOD, axis=-1)
    h = rmsnorm(x, norm_mix) * (1 + sc1) + sh1
    proj = h @ w_in
    q, k, v, conv_x, gate_b, gate_c, p_in, sg_u, sg_v, gates = jnp.split(
        proj, [BRANCH_W * i for i in range(1, 10)], axis=-1)
    q, k, v = (t.reshape(B, T, SB_HEADS, SB_HEAD_DIM) for t in (q, k, v))
    if past is None:
        y_a = sb_prompt(q, k, v, sb_bias)
        conv_prev = jnp.zeros((B, CONV_W - 1, BRANCH_W), x.dtype)
        pool_prev = jnp.zeros((B, POOL_PAD, BRANCH_W), x.dtype)
    else:
        k_past, v_past, conv_prev, pool_prev = past
        k_all = jnp.concatenate([k_past.astype(k.dtype), k], axis=1)
        v_all = jnp.concatenate([v_past.astype(v.dtype), v], axis=1)
        y_a = sb_attend(q, k_all, v_all, sb_bias, k_past.shape[1])
    y_a = y_a.reshape(B, T, BRANCH_W)
    conv_out, conv_state = short_conv(gate_c * conv_x, conv_prev, conv_w)
    y_b = gate_b * conv_out
    y_c, pool_state = pool_mix(p_in, pool_prev, pos0, pool_w, pool_scale)
    y_d, sg_state = spatial_gate(sg_u, sg_v, sg_w, sg_b, sg_norm)
    gates = jax.nn.sigmoid(gates.reshape(B, T, N_BRANCH, D_MODEL))
    merged = sum(gates[:, :, i] * (y @ w_branch[i]) for i, y in enumerate((y_a, y_b, y_c, y_d)))
    x = x + g1 * (merged @ w_o)
    h2 = rmsnorm(x, norm_ffn) * (1 + sc2) + sh2
    x = x + g2 * (jnp.square(jax.nn.relu(h2 @ w_ff1)) @ w_ff2)
    return x, (k, v, conv_state, pool_state, sg_state)


def setup_inputs(seed: int = 0) -> dict:
    key = jax.random.key(seed)
    ks = jax.random.split(key, 32)
    n_pages = PAST_LEN // PAGE_SIZE
    n_phys = (DEC_BATCH * n_pages * 5) // 4
    f32 = jnp.float32

    def nrm(k, shape, s):
        return jax.random.normal(k, shape, f32) * s

    page_table = jax.random.permutation(ks[7], n_phys)[:DEC_BATCH * n_pages].reshape(
        DEC_BATCH, n_pages).astype(jnp.int32)
    return {
        'x_prompt': nrm(ks[0], (BATCH, SEQ, D_MODEL), 1.0),
        'x_sample': nrm(ks[1], (DEC_BATCH, DEC_SEQ, D_MODEL), 1.0),
        'cache_k': nrm(ks[2], (DEPTH, n_phys, PAGE_SIZE, SB_HEADS, SB_HEAD_DIM), 1.0),
        'cache_v': nrm(ks[3], (DEPTH, n_phys, PAGE_SIZE, SB_HEADS, SB_HEAD_DIM), 1.0),
        'state_conv': nrm(ks[4], (DEPTH, DEC_BATCH, CONV_W - 1, BRANCH_W), 1.0),
        'state_pool': nrm(ks[5], (DEPTH, DEC_BATCH, POOL_PAD, BRANCH_W), 1.0),
        'page_table': page_table,
        'c_prompt': nrm(ks[8], (BATCH, D_MODEL), 1.0),
        'c_sample': nrm(ks[9], (DEC_BATCH, D_MODEL), 1.0),
        'w_ada': nrm(ks[10], (DEPTH, D_MODEL, N_MOD * D_MODEL), 0.5 * D_MODEL ** -0.5),
        'b_ada': nrm(ks[11], (DEPTH, N_MOD * D_MODEL), 0.02),
        'norm_mix': 1.0 + nrm(ks[12], (DEPTH, D_MODEL), 0.02),
        'norm_ffn': 1.0 + nrm(ks[13], (DEPTH, D_MODEL), 0.02),
        'w_in': nrm(ks[14], (DEPTH, D_MODEL, D_IN), D_MODEL ** -0.5),
        'sb_bias': SB_BIAS_INIT + nrm(ks[26], (DEPTH, SB_HEADS), 0.1),
        'conv_w': nrm(ks[15], (DEPTH, CONV_W, BRANCH_W), CONV_W ** -0.5),
        'pool_w': nrm(ks[16], (DEPTH, len(POOL_WINDOWS), POOL_GROUP, POOL_GROUP), POOL_GROUP ** -0.5),
        'pool_scale': 1.0 + nrm(ks[17], (DEPTH, BRANCH_W), 0.1),
        'sg_w': nrm(ks[18], (DEPTH, SG_GROUPS, SG_CHUNK, SG_CHUNK), SG_CHUNK ** -0.5),
        'sg_b': 1.0 + nrm(ks[19], (DEPTH, SG_GROUPS, SG_CHUNK), 0.02),
        'sg_norm': 1.0 + nrm(ks[20], (DEPTH, BRANCH_W), 0.02),
        'w_branch': nrm(ks[21], (DEPTH, N_BRANCH, BRANCH_W, D_MODEL), BRANCH_W ** -0.5),
        'w_o': nrm(ks[22], (DEPTH, D_MODEL, D_MODEL), D_MODEL ** -0.5),
        'w_ff1': nrm(ks[23], (DEPTH, D_MODEL, D_FF), D_MODEL ** -0.5),
        'w_ff2': nrm(ks[24], (DEPTH, D_FF, D_MODEL), D_FF ** -0.5),
        'norm_final': 1.0 + nrm(ks[25], (D_MODEL,), 0.02),
    }


def reference(x_prompt, x_sample, cache_k, cache_v, state_conv, state_pool, page_table,
              c_prompt, c_sample, w_ada, b_ada, norm_mix, norm_ffn, w_in, sb_bias, conv_w, pool_w,
              pool_scale, sg_w, sg_b, sg_norm, w_branch, w_o, w_ff1, w_ff2, norm_final):
    past_len = page_table.shape[1] * cache_k.shape[2]
    xp, xs = x_prompt, x_sample
    st_p = ([], [], [], [])
    st_s = ([], [], [], [], [])
    for l in range(DEPTH):
        lw = (w_ada[l], b_ada[l], norm_mix[l], norm_ffn[l], w_in[l], sb_bias[l], conv_w[l], pool_w[l],
              pool_scale[l], sg_w[l], sg_b[l], sg_norm[l], w_branch[l], w_o[l], w_ff1[l], w_ff2[l])
        xp, new_p = trunk_layer(xp, c_prompt, None, 0, *lw)
        past = (gather_pages(cache_k[l], page_table), gather_pages(cache_v[l], page_table),
                state_conv[l], state_pool[l])
        xs, new_s = trunk_layer(xs, c_sample, past, past_len, *lw)
        for acc, t in zip(st_p, new_p[:4]):
            acc.append(t)
        for acc, t in zip(st_s, new_s):
            acc.append(t)
    y_prompt = rmsnorm(xp, norm_final)
    y_sample = rmsnorm(xs, norm_final)
    k_prompt, v_prompt, conv_prompt, pool_prompt = (jnp.stack(a) for a in st_p)
    k_sample, v_sample, conv_sample, pool_sample, sgv_sample = (jnp.stack(a) for a in st_s)
    return (y_prompt, y_sample, k_prompt, v_prompt, conv_prompt, pool_prompt,
            k_sample, v_sample, conv_sample, pool_sample, sgv_sample)
```

```python
import functools

import jax
import jax.numpy as jnp
from jax import lax
from jax.experimental import pallas as pl
from jax.experimental.pallas import tpu as pltpu

F32 = jnp.float32
BF16 = jnp.bfloat16

NORM_EPS = 1e-6
CONV_TAPS = 3
POOL_WINDOWS = (2, 4, 8, 16)
POOL_PAD = max(POOL_WINDOWS) - 1
SG_CHUNK = 128
N_MOD = 6
N_BRANCH = 4

LANES = 128
SUBLANES = 8
KEY_BLOCK = 128
Q_TILE = 256
VMEM_LIMIT_BYTES = 56 * 1024 * 1024


def _params(n_axes):
    return pltpu.CompilerParams(
        dimension_semantics=("arbitrary",) * n_axes, vmem_limit_bytes=VMEM_LIMIT_BYTES)


def _dot(a, b):
    return jnp.dot(a, b, preferred_element_type=F32)


def _dot_nt(a, b):
    return lax.dot_general(a, b, (((1,), (1,)), ((), ())), preferred_element_type=F32)


def _sigmoid(x):
    return 1.0 / (1.0 + jnp.exp(-x))


def _rms(x, g):
    return x * lax.rsqrt(jnp.mean(x * x, axis=-1, keepdims=True) + NORM_EPS) * g


def _norm_mod(x, g, sc, sh):
    return _rms(x, g) * (1.0 + sc) + sh


def _adaln_kernel(c_ref, w_ref, b_ref, os_ref, op_ref):
    c = c_ref[...]
    a = (c * _sigmoid(c)).astype(BF16)
    r = _dot(a, w_ref[...].astype(BF16)) + b_ref[...]
    ns = os_ref.shape[0]
    os_ref[...] = r[:ns]
    op_ref[...] = r[ns:]


def _adaln(c_all, w_ada, b_ada, n_sample, n_prompt_pad):
    depth, d, n = w_ada.shape
    tn = 1024
    rows = c_all.shape[0]
    return pl.pallas_call(
        _adaln_kernel,
        grid=(depth, n // tn),
        in_specs=[
            pl.BlockSpec((rows, d), lambda l, j: (0, 0)),
            pl.BlockSpec((None, d, tn), lambda l, j: (l, 0, j)),
            pl.BlockSpec((None, 1, tn), lambda l, j: (l, 0, j)),
        ],
        out_specs=[
            pl.BlockSpec((None, n_sample, tn), lambda l, j: (l, 0, j)),
            pl.BlockSpec((None, n_prompt_pad, tn), lambda l, j: (l, 0, j)),
        ],
        out_shape=[
            jax.ShapeDtypeStruct((depth, n_sample, n), F32),
            jax.ShapeDtypeStruct((depth, n_prompt_pad, n), F32),
        ],
        compiler_params=_params(2),
        name="adaln",
    )(c_all, w_ada, b_ada.reshape(depth, 1, n))


class _Group:
    def __init__(self, kind, n_tiles, tm, d, tiles_per_seq=None):
        self.kind, self.n_tiles, self.tm, self.d, self.tps = kind, n_tiles, tm, d, tiles_per_seq

    def x_spec(self, width, col_blocks=1, col=0):
        if self.kind == "prompt":
            return pl.BlockSpec((self.tm, width), lambda i, *_: (i, col))
        return pl.BlockSpec((self.tm, width), lambda i, *_: (0, i * col_blocks + col))

    def x_spec_j(self, width, col_blocks):
        if self.kind == "prompt":
            return pl.BlockSpec((self.tm, width), lambda i, j: (i, j))
        return pl.BlockSpec((self.tm, width), lambda i, j: (0, i * col_blocks + j))

    def mod_spec(self, layer, chunk):
        if self.kind == "prompt":
            tps = self.tps
            return pl.BlockSpec((None, None, 1, self.d), lambda i, *_: (layer, i // tps, 0, chunk))
        return pl.BlockSpec((None, self.tm, self.d), lambda i, *_: (layer, 0, chunk))


def _const_spec(shape):
    nd = len(shape)
    return pl.BlockSpec(shape, lambda *_: (0,) * nd)


def _layer_spec(shape, layer):
    nd = len(shape)
    return pl.BlockSpec((None,) + tuple(shape), lambda *_: (layer,) + (0,) * nd)


def _qkv_kernel(x_ref, sh_ref, sc_ref, g_ref, w_ref, q_ref, k_ref, v_ref, *, q_scale):
    h = _norm_mod(x_ref[...], g_ref[...], sc_ref[...], sh_ref[...]).astype(BF16)
    p = _dot(h, w_ref[...])
    bw = q_ref.shape[1]
    q_ref[...] = (p[:, :bw] * q_scale).astype(q_ref.dtype)
    k_ref[...] = p[:, bw:2 * bw]
    v_ref[...] = p[:, 2 * bw:3 * bw]


def _qkv_proj(grp, layer, x, mod, norm_w, w_qkv, bw, q_scale):
    d = grp.d
    rows, cols = x.shape
    out_cols = cols // d * bw
    o_spec = grp.x_spec(bw)
    return pl.pallas_call(
        functools.partial(_qkv_kernel, q_scale=q_scale),
        grid=(grp.n_tiles,),
        in_specs=[
            grp.x_spec(d),
            grp.mod_spec(layer, 0),
            grp.mod_spec(layer, 1),
            _layer_spec((1, d), layer),
            _layer_spec((d, 3 * bw), layer),
        ],
        out_specs=[o_spec, o_spec, o_spec],
        out_shape=[
            jax.ShapeDtypeStruct((rows, out_cols), BF16),
            jax.ShapeDtypeStruct((rows, out_cols), F32),
            jax.ShapeDtypeStruct((rows, out_cols), F32),
        ],
        compiler_params=_params(1),
        name="qkv_proj",
    )(x, mod, mod, norm_w, w_qkv)


def _rest_kernel(x_ref, sh_ref, sc_ref, g_ref, w_ref, o_ref, h_scr):
    @pl.when(pl.program_id(1) == 0)
    def _():
        h_scr[...] = _norm_mod(x_ref[...], g_ref[...], sc_ref[...], sh_ref[...]).astype(BF16)

    o_ref[...] = _dot(h_scr[...], w_ref[...])


def _rest_proj(grp, layer, x, mod, norm_w, w_rest, tn):
    d = grp.d
    rows, cols = x.shape
    n_rest = w_rest.shape[-1]
    nj = n_rest // tn
    return pl.pallas_call(
        _rest_kernel,
        grid=(grp.n_tiles, nj),
        in_specs=[
            grp.x_spec(d),
            grp.mod_spec(layer, 0),
            grp.mod_spec(layer, 1),
            _layer_spec((1, d), layer),
            pl.BlockSpec((None, d, tn), lambda i, j: (layer, 0, j)),
        ],
        out_specs=grp.x_spec_j(tn, nj),
        out_shape=jax.ShapeDtypeStruct((rows, cols // d * n_rest), F32),
        scratch_shapes=[pltpu.VMEM((grp.tm, d), BF16)],
        compiler_params=_params(2),
        name="rest_proj",
    )(x, mod, mod, norm_w, w_rest)


def _suffix_prod_sublanes(x):
    sub = lax.broadcasted_iota(jnp.int32, x.shape, 0)
    y = x
    for k in (1, 2, 4):
        shifted = pltpu.roll(y, SUBLANES - k, axis=0)
        y = y * jnp.where(sub + k < SUBLANES, shifted, 1.0)
    return y


def _shift_up_sublanes(y):
    sub = lax.broadcasted_iota(jnp.int32, y.shape, 0)
    return jnp.where(sub + 1 < SUBLANES, pltpu.roll(y, SUBLANES - 1, axis=0), 1.0)


def _sb_tile_permuted(s_tile, bias_half, carry, diag_offset):
    n = s_tile.shape[1]
    g = 0.5 * jnp.tanh(s_tile + bias_half)
    beta = 0.5 + g
    rest = 0.5 - g
    if diag_offset is not None:
        r = lax.broadcasted_iota(jnp.int32, s_tile.shape, 0)
        q_idx = lax.broadcasted_iota(jnp.int32, s_tile.shape, 1)
        k_idx = (r & 7) * 16 + (r >> 3) + diag_offset
        valid = k_idx < q_idx
        beta = jnp.where(valid, beta, 0.0)
        rest = jnp.where(valid, rest, 1.0)
    nv = KEY_BLOCK // SUBLANES
    rs = [rest[v * SUBLANES:(v + 1) * SUBLANES] for v in range(nv)]
    later = [None] * nv
    acc = None
    for v in range(nv - 1, -1, -1):
        later[v] = acc
        acc = rs[v] if acc is None else acc * rs[v]
    incl = _suffix_prod_sublanes(acc)
    base = _shift_up_sublanes(incl) * carry
    ws = []
    for v in range(nv):
        p = base if later[v] is None else later[v] * base
        ws.append(beta[v * SUBLANES:(v + 1) * SUBLANES] * p)
    w = jnp.concatenate(ws, axis=0).astype(BF16)
    return w, carry * incl[0:1]


def _attn_prompt_kernel(bias_ref, q_ref, k_ref, v_ref, o_ref, kp_scr, vt_scr, wt_scr, acc_scr, *, hd):
    hp = pl.program_id(1)
    i = pl.program_id(2)
    tq = q_ref.shape[0]
    n_chunks = k_ref.shape[0] // tq
    halves = tq // KEY_BLOCK

    @pl.when(i == 0)
    def _():
        r = lax.broadcasted_iota(jnp.int32, (KEY_BLOCK, KEY_BLOCK), 0)
        j = lax.broadcasted_iota(jnp.int32, (KEY_BLOCK, KEY_BLOCK), 1)
        perm = jnp.where(j == (r & 7) * 16 + (r >> 3), 1.0, 0.0).astype(BF16)

        def body(c, carry):
            for hf in range(halves):
                rows = pl.ds(pl.multiple_of(c * tq + hf * KEY_BLOCK, KEY_BLOCK), KEY_BLOCK)
                kp = _dot(perm, k_ref[rows, :].astype(BF16))
                vp = _dot(perm, v_ref[rows, :].astype(BF16))
                kp_scr[c, hf * KEY_BLOCK:(hf + 1) * KEY_BLOCK, :] = kp.astype(BF16)
                vt_scr[c, :, hf * KEY_BLOCK:(hf + 1) * KEY_BLOCK] = vp.T.astype(BF16)
            return carry

        lax.fori_loop(0, n_chunks, body, 0)

    q = q_ref[...]
    lane = lax.broadcasted_iota(jnp.int32, q.shape, 1)
    zero = jnp.zeros_like(q)
    qm = (jnp.where(lane < hd, q, zero), jnp.where(lane >= hd, q, zero))
    bias_half = (0.5 * bias_ref[2 * hp], 0.5 * bias_ref[2 * hp + 1])
    acc_scr[...] = jnp.zeros_like(acc_scr)

    def chunk(c, diag, carry):
        k_blk = kp_scr[c]
        vt_blk = vt_scr[c]
        out = []
        for h in range(2):
            s = _dot_nt(k_blk, qm[h])
            cr = carry[h]
            for hf in range(halves - 1, -1, -1):
                w, cr = _sb_tile_permuted(
                    s[hf * KEY_BLOCK:(hf + 1) * KEY_BLOCK], bias_half[h], cr,
                    hf * KEY_BLOCK if diag else None)
                wt_scr[h, hf * KEY_BLOCK:(hf + 1) * KEY_BLOCK, :] = w
            out.append(cr)
            acc_scr[h] += _dot(vt_blk, wt_scr[h])
        return tuple(out)

    ones = jnp.ones((1, tq), F32)
    carry = chunk(i, True, (ones, ones))
    lax.fori_loop(0, i, lambda t, cr: chunk(i - 1 - t, False, cr), carry)

    row = lax.broadcasted_iota(jnp.int32, (2 * hd, tq), 0)
    res = jnp.where(row < hd, acc_scr[0], acc_scr[1])
    o_ref[...] = res.T.astype(o_ref.dtype)


def _attn_prompt(q, k, v, bias, hd):
    b, t, c = q.shape
    assert 2 * hd == LANES and t % Q_TILE == 0
    n_chunks = t // Q_TILE
    qo_spec = pl.BlockSpec((None, Q_TILE, LANES), lambda bi, hp, i: (bi, i, hp))
    kv_spec = pl.BlockSpec((None, t, LANES), lambda bi, hp, i: (bi, 0, hp))
    return pl.pallas_call(
        functools.partial(_attn_prompt_kernel, hd=hd),
        grid=(b, c // LANES, n_chunks),
        in_specs=[pl.BlockSpec(memory_space=pltpu.SMEM), qo_spec, kv_spec, kv_spec],
        out_specs=qo_spec,
        out_shape=jax.ShapeDtypeStruct((b, t, c), BF16),
        scratch_shapes=[
            pltpu.VMEM((n_chunks, Q_TILE, LANES), BF16),
            pltpu.VMEM((n_chunks, LANES, Q_TILE), BF16),
            pltpu.VMEM((2, Q_TILE, Q_TILE), BF16),
            pltpu.VMEM((2, LANES, Q_TILE), F32),
        ],
        compiler_params=_params(3),
        name="attn_prompt",
    )(bias, q, k, v)


def _sb_tile_natural(s_tile, bias_half, carry, valid):
    g = 0.5 * jnp.tanh(s_tile + bias_half)
    beta = 0.5 + g
    rest = 0.5 - g
    if valid is not None:
        beta = jnp.where(valid, beta, 0.0)
        rest = jnp.where(valid, rest, 1.0)
    nv = KEY_BLOCK // SUBLANES
    ws = [None] * nv
    after = carry
    for v in range(nv - 1, -1, -1):
        sl = slice(v * SUBLANES, (v + 1) * SUBLANES)
        incl = _suffix_prod_sublanes(rest[sl])
        ws[v] = beta[sl] * (_shift_up_sublanes(incl) * after)
        after = after * incl[0:1]
    return jnp.concatenate(ws, axis=0), after


def _attn_decode_kernel(pt_ref, bias_ref, q_ref, kn_ref, vn_ref, *rest, hd, n_pages):
    del pt_ref
    k_pages = rest[:n_pages]
    v_pages = rest[n_pages:2 * n_pages]
    o_ref = rest[2 * n_pages]
    s_len, c = q_ref.shape
    n_heads = c // hd
    ncol = LANES

    head_row = lax.broadcasted_iota(jnp.int32, (n_heads, c), 0)
    lane = lax.broadcasted_iota(jnp.int32, (n_heads, c), 1)
    head_mask = (lane >= head_row * hd) & (lane < (head_row + 1) * hd)

    q = q_ref[...].astype(F32)
    blocks = [jnp.where(head_mask, jnp.broadcast_to(q[t:t + 1], (n_heads, c)), 0.0)
              for t in range(s_len)]
    blocks.append(jnp.zeros((ncol - s_len * n_heads, c), F32))
    q_cols = jnp.concatenate(blocks, axis=0).astype(BF16)
    bias_half = 0.5 * bias_ref[...]

    def tile(k_tile, v_tile, valid, carry, acc):
        s = _dot_nt(k_tile, q_cols)
        w, carry = _sb_tile_natural(s, bias_half, carry, valid)
        acc = acc + _dot(w.T.astype(BF16), v_tile)
        return carry, acc

    pad = jnp.zeros((KEY_BLOCK - s_len, c), F32)
    k_new = jnp.concatenate([kn_ref[...], pad], axis=0).astype(BF16)
    v_new = jnp.concatenate([vn_ref[...], pad], axis=0).astype(BF16)
    key = lax.broadcasted_iota(jnp.int32, (KEY_BLOCK, ncol), 0)
    col = lax.broadcasted_iota(jnp.int32, (KEY_BLOCK, ncol), 1)
    carry = jnp.ones((1, ncol), F32)
    acc = jnp.zeros((ncol, c), F32)
    carry, acc = tile(k_new, v_new, key * n_heads + n_heads <= col, carry, acc)
    for p in range(n_pages - 1, -1, -1):
        carry, acc = tile(k_pages[p][...].astype(BF16), v_pages[p][...].astype(BF16), None, carry, acc)

    outs = [jnp.sum(jnp.where(head_mask, acc[t * n_heads:(t + 1) * n_heads], 0.0), axis=0, keepdims=True)
            for t in range(s_len)]
    o_ref[...] = jnp.concatenate(outs, axis=0)


def _attn_decode(layer, q, k_new, v_new, cache_k, cache_v, page_table, bias_cols, hd):
    nb, s_len, c = q.shape
    n_pages = page_table.shape[1]
    page = cache_k.shape[2]
    assert page == KEY_BLOCK and c // hd == SUBLANES and s_len * (c // hd) <= LANES
    seq_spec = pl.BlockSpec((None, s_len, c), lambda b, pt: (b, 0, 0))

    def page_spec(p):
        return pl.BlockSpec((None, None, page, c), lambda b, pt: (layer, pt[b, p], 0, 0))

    grid_spec = pltpu.PrefetchScalarGridSpec(
        num_scalar_prefetch=1,
        grid=(nb,),
        in_specs=[pl.BlockSpec((1, LANES), lambda b, pt: (0, 0)), seq_spec, seq_spec, seq_spec]
        + [page_spec(p) for p in range(n_pages)] * 2,
        out_specs=seq_spec,
    )
    return pl.pallas_call(
        functools.partial(_attn_decode_kernel, hd=hd, n_pages=n_pages),
        grid_spec=grid_spec,
        out_shape=jax.ShapeDtypeStruct((nb, s_len, c), F32),
        compiler_params=_params(1),
        name="attn_decode",
    )(page_table, bias_cols, q, k_new, v_new, *([cache_k] * n_pages), *([cache_v] * n_pages))


def _merge(x, ys, gates, wb_ref, wo_ref, g1):
    merged = None
    for i, (y, gate) in enumerate(zip(ys, gates)):
        term = _sigmoid(gate) * _dot(y.astype(BF16), wb_ref[i])
        merged = term if merged is None else merged + term
    return x + g1 * _dot(merged.astype(BF16), wo_ref[...])


def _pool_count(pos, win):
    return jnp.minimum(pos + 1.0, float(win))


def _mixer_prompt_kernel(x_ref, ya_ref, cv_ref, cvh_ref, pi_ref, pih_ref, su_ref, sv_ref,
                         g0_ref, g1_ref, g2_ref, g3_ref, cw_ref, pw_ref, ps_ref, sw_ref, sb_ref,
                         sn_ref, wb_ref, wo_ref, m1_ref, o_ref, cst_ref, pst_ref, cbuf, pbuf,
                         *, bw, tiles_per_seq):
    tm = x_ref.shape[0]
    first = (pl.program_id(0) % tiles_per_seq) == 0
    hc = cvh_ref.shape[0]
    hp = pih_ref.shape[0]

    cv = cv_ref[...]
    cx = cv[:, 2 * bw:3 * bw] * cv[:, 0:bw]
    cvh = cvh_ref[...]
    cbuf[0:hc, :] = jnp.where(first, 0.0, cvh[:, 2 * bw:3 * bw] * cvh[:, 0:bw])
    cbuf[hc:hc + tm, :] = cx
    cw = cw_ref[...]
    conv = cw[0:1] * cbuf[pl.ds(hc - 2, tm), :] + cw[1:2] * cbuf[pl.ds(hc - 1, tm), :] + cw[2:3] * cx
    y_b = cv[:, bw:2 * bw] * conv
    cst_ref[...] = cx[tm - hc:tm]

    pin = pi_ref[...]
    pbuf[0:hp, :] = jnp.where(first, 0.0, pih_ref[...])
    pbuf[hp:hp + tm, :] = pin
    pst_ref[...] = pin[tm - hp:tm]
    pos = (lax.broadcasted_iota(jnp.int32, (tm, 1), 0)
           + (pl.program_id(0) % tiles_per_seq) * tm).astype(F32)
    gw = bw // len(POOL_WINDOWS)
    y_c = []
    for g, win in enumerate(POOL_WINDOWS):
        lanes = slice(g * gw, (g + 1) * gw)
        wsum = pin[:, lanes]
        for kk in range(1, win):
            wsum = wsum + pbuf[pl.ds(hp - kk, tm), lanes]
        diff = wsum / _pool_count(pos, win) - pin[:, lanes]
        y_c.append(_dot(diff.astype(BF16), pw_ref[g]))
    y_c = jnp.concatenate(y_c, axis=1) * ps_ref[...]

    vn = _rms(sv_ref[...], sn_ref[...])
    n_groups = sw_ref.shape[0] // SG_CHUNK
    gwd = bw // n_groups
    r = lax.broadcasted_iota(jnp.int32, sw_ref.shape, 0) & (SG_CHUNK - 1)
    s_idx = lax.broadcasted_iota(jnp.int32, sw_ref.shape, 1)
    w_tril = jnp.where(s_idx <= r, sw_ref[...], 0.0).astype(BF16)
    lane_grp = lax.broadcasted_iota(jnp.int32, (SG_CHUNK, bw), 1)
    mixed = []
    for n in range(tm // SG_CHUNK):
        full = _dot(w_tril, vn[n * SG_CHUNK:(n + 1) * SG_CHUNK].astype(BF16))
        m = None
        for g in range(n_groups):
            blk = full[g * SG_CHUNK:(g + 1) * SG_CHUNK]
            sel = jnp.where((lane_grp >= g * gwd) & (lane_grp < (g + 1) * gwd), blk, 0.0)
            m = sel if m is None else m + sel
        mixed.append(m + sb_ref[...])
    y_d = su_ref[...] * jnp.concatenate(mixed, axis=0)

    gates = (g0_ref[...], g1_ref[...], g2_ref[...], g3_ref[...])
    o_ref[...] = _merge(x_ref[...], (ya_ref[...], y_b, y_c, y_d), gates, wb_ref, wo_ref, m1_ref[...])


def _mixer_prompt(grp, layer, x, ya, rest, mod, conv_w, pool_w, pool_scale, sg_w, sg_bt, sg_norm,
                  w_branch, w_o, bw, n_seq):
    d, tm, tps = grp.d, grp.tm, grp.tps
    rows = x.shape[0]
    hc, hp = SUBLANES, 2 * SUBLANES
    assert hc >= CONV_TAPS - 1 and hp >= POOL_PAD and tm % SG_CHUNK == 0
    gate_col = 6 * bw // d

    def halo(h, width, col):
        per = tm // h
        return pl.BlockSpec((h, width), lambda i: (jnp.maximum(i * per - 1, 0), col))

    n_groups = sg_w.shape[1]
    return pl.pallas_call(
        functools.partial(_mixer_prompt_kernel, bw=bw, tiles_per_seq=tps),
        grid=(grp.n_tiles,),
        in_specs=[
            grp.x_spec(d), grp.x_spec(bw),
            grp.x_spec(3 * bw, col=0), halo(hc, 3 * bw, 0),
            grp.x_spec(bw, col=3), halo(hp, bw, 3),
            grp.x_spec(bw, col=4), grp.x_spec(bw, col=5),
            grp.x_spec(d, col=gate_col), grp.x_spec(d, col=gate_col + 1),
            grp.x_spec(d, col=gate_col + 2), grp.x_spec(d, col=gate_col + 3),
            _layer_spec((CONV_TAPS, bw), layer),
            _layer_spec(pool_w.shape[1:], layer),
            _layer_spec((1, bw), layer),
            _layer_spec((n_groups * SG_CHUNK, SG_CHUNK), layer),
            _layer_spec((SG_CHUNK, bw), layer),
            _layer_spec((1, bw), layer),
            _layer_spec(w_branch.shape[1:], layer),
            _layer_spec(w_o.shape[1:], layer),
            grp.mod_spec(layer, 2),
        ],
        out_specs=[
            grp.x_spec(d),
            pl.BlockSpec((None, hc, bw), lambda i: (i // tps, 0, 0)),
            pl.BlockSpec((None, hp, bw), lambda i: (i // tps, 0, 0)),
        ],
        out_shape=[
            jax.ShapeDtypeStruct((rows, d), F32),
            jax.ShapeDtypeStruct((n_seq, hc, bw), F32),
            jax.ShapeDtypeStruct((n_seq, hp, bw), F32),
        ],
        scratch_shapes=[pltpu.VMEM((hc + tm, bw), F32), pltpu.VMEM((hp + tm, bw), F32)],
        compiler_params=_params(1),
        name="mixer_prompt",
    )(x, ya, rest, rest, rest, rest, rest, rest, rest, rest, rest, rest,
      conv_w, pool_w, pool_scale, sg_w.reshape(sg_w.shape[0], n_groups * SG_CHUNK, SG_CHUNK),
      sg_bt, sg_norm, w_branch, w_o, mod)


def _mixer_decode_kernel(x_ref, ya_ref, r_ref, cs_ref, ps_ref, cw_ref, pw_ref, psc_ref, swv_ref,
                         sbv_ref, sn_ref, wb_ref, wo_ref, m1_ref,
                         o_ref, cst_ref, pst_ref, sgv_ref, *, bw, d, s_len, pos0):
    n_rest = r_ref.shape[1] // s_len

    def rest(t, off, width):
        return r_ref[:, t * n_rest + off:t * n_rest + off + width]

    cw = cw_ref[...]
    cx = [rest(t, 2 * bw, bw) * rest(t, 0, bw) for t in range(s_len)]
    taps = CONV_TAPS - 1
    xp = [cs_ref[:, i * bw:(i + 1) * bw] for i in range(taps)] + cx
    y_b = []
    for t in range(s_len):
        conv = cw[0:1] * xp[t]
        for i in range(1, CONV_TAPS):
            conv = conv + cw[i:i + 1] * xp[t + i]
        y_b.append(rest(t, bw, bw) * conv)
    for i in range(taps):
        cst_ref[:, i * bw:(i + 1) * bw] = xp[len(xp) - taps + i]

    pin = [rest(t, 3 * bw, bw) for t in range(s_len)]
    pp = [ps_ref[:, i * bw:(i + 1) * bw] for i in range(POOL_PAD)] + pin
    for i in range(POOL_PAD):
        pst_ref[:, i * bw:(i + 1) * bw] = pp[len(pp) - POOL_PAD + i]
    gw = bw // len(POOL_WINDOWS)
    diffs = [[] for _ in POOL_WINDOWS]
    for t in range(s_len):
        for g, win in enumerate(POOL_WINDOWS):
            lanes = slice(g * gw, (g + 1) * gw)
            wsum = pp[POOL_PAD + t][:, lanes]
            for kk in range(1, win):
                wsum = wsum + pp[POOL_PAD + t - kk][:, lanes]
            cnt = float(min(pos0 + t + 1, win))
            diffs[g].append(wsum / cnt - pin[t][:, lanes])
    y_c_groups = [_dot(jnp.concatenate(diffs[g], axis=0).astype(BF16), pw_ref[g])
                  for g in range(len(POOL_WINDOWS))]
    y_c_all = jnp.concatenate(y_c_groups, axis=1) * psc_ref[...]

    tb = x_ref.shape[0]
    vn = [_rms(rest(t, 5 * bw, bw), sn_ref[...]) for t in range(s_len)]
    y_d = []
    for t in range(s_len):
        sgv_ref[:, t * bw:(t + 1) * bw] = vn[t]
        mixed = sbv_ref[t:t + 1]
        for s in range(t + 1):
            mixed = mixed + swv_ref[t * s_len + s:t * s_len + s + 1] * vn[s]
        y_d.append(rest(t, 4 * bw, bw) * mixed)

    cat = lambda parts: jnp.concatenate(parts, axis=0)
    x_all = cat([x_ref[:, t * d:(t + 1) * d] for t in range(s_len)])
    ya_all = cat([ya_ref[:, t * bw:(t + 1) * bw] for t in range(s_len)])
    gates = [cat([rest(t, 6 * bw + i * d, d) for t in range(s_len)]) for i in range(N_BRANCH)]
    g1 = cat([m1_ref[...]] * s_len)
    out = _merge(x_all, (ya_all, cat(y_b), y_c_all, cat(y_d)), gates, wb_ref, wo_ref, g1)
    for t in range(s_len):
        o_ref[:, t * d:(t + 1) * d] = out[t * tb:(t + 1) * tb]


def _mixer_decode(layer, x, ya, rest, state_conv, state_pool, mod, conv_w, pool_w, pool_scale,
                  sg_wv, sg_bv, sg_norm, w_branch, w_o, bw, d, s_len, pos0, tb):
    nb = x.shape[0]

    def row_spec(width):
        return pl.BlockSpec((tb, width), lambda i: (i, 0))

    def state_spec(width):
        return pl.BlockSpec((None, tb, width), lambda i: (layer, i, 0))

    return pl.pallas_call(
        functools.partial(_mixer_decode_kernel, bw=bw, d=d, s_len=s_len, pos0=pos0),
        grid=(nb // tb,),
        in_specs=[
            row_spec(s_len * d), row_spec(s_len * bw), row_spec(rest.shape[1]),
            state_spec(state_conv.shape[2]), state_spec(state_pool.shape[2]),
            _layer_spec((CONV_TAPS, bw), layer),
            _layer_spec(pool_w.shape[1:], layer),
            _layer_spec((1, bw), layer),
            _layer_spec(sg_wv.shape[1:], layer),
            _layer_spec(sg_bv.shape[1:], layer),
            _layer_spec((1, bw), layer),
            _layer_spec(w_branch.shape[1:], layer),
            _layer_spec(w_o.shape[1:], layer),
            pl.BlockSpec((None, tb, d), lambda i: (layer, i, 2)),
        ],
        out_specs=[row_spec(s_len * d), row_spec(state_conv.shape[2]), row_spec(state_pool.shape[2]),
                   row_spec(s_len * bw)],
        out_shape=[
            jax.ShapeDtypeStruct((nb, s_len * d), F32),
            jax.ShapeDtypeStruct((nb, state_conv.shape[2]), F32),
            jax.ShapeDtypeStruct((nb, state_pool.shape[2]), F32),
            jax.ShapeDtypeStruct((nb, s_len * bw), F32),
        ],
        compiler_params=_params(1),
        name="mixer_decode",
    )(x, ya, rest, state_conv, state_pool, conv_w, pool_w, pool_scale, sg_wv, sg_bv, sg_norm,
      w_branch, w_o, mod)


def _ffn_kernel(x_ref, sh_ref, sc_ref, gt_ref, g_ref, w1_ref, w2_ref, fw_ref, o_ref, h_scr, acc_scr,
                *, final_norm):
    j = pl.program_id(1)

    @pl.when(j == 0)
    def _():
        h_scr[...] = _norm_mod(x_ref[...], g_ref[...], sc_ref[...], sh_ref[...]).astype(BF16)
        acc_scr[...] = jnp.zeros_like(acc_scr)

    a = jnp.maximum(_dot(h_scr[...], w1_ref[...]), 0.0)
    acc_scr[...] += _dot((a * a).astype(BF16), w2_ref[...])

    @pl.when(j == pl.num_programs(1) - 1)
    def _():
        y = x_ref[...] + gt_ref[...] * acc_scr[...]
        if final_norm:
            y = _rms(y, fw_ref[...])
        o_ref[...] = y


def _ffn(grp, layer, x, mod, norm_w, w1, w2, final_w, final_norm, tf):
    d = grp.d
    dff = w1.shape[-1]
    return pl.pallas_call(
        functools.partial(_ffn_kernel, final_norm=final_norm),
        grid=(grp.n_tiles, dff // tf),
        in_specs=[
            grp.x_spec(d),
            grp.mod_spec(layer, 3), grp.mod_spec(layer, 4), grp.mod_spec(layer, 5),
            _layer_spec((1, d), layer),
            pl.BlockSpec((None, d, tf), lambda i, j: (layer, 0, j)),
            pl.BlockSpec((None, tf, d), lambda i, j: (layer, j, 0)),
            _const_spec((1, d)),
        ],
        out_specs=grp.x_spec(d),
        out_shape=jax.ShapeDtypeStruct(x.shape, F32),
        scratch_shapes=[pltpu.VMEM((grp.tm, d), BF16), pltpu.VMEM((grp.tm, d), F32)],
        compiler_params=_params(2),
        name="ffn",
    )(x, mod, mod, mod, norm_w, w1, w2, final_w)


def _row_tile(t, want):
    tm = min(t, want)
    assert t % tm == 0
    return tm


def kernel(x_prompt, x_sample, cache_k, cache_v, state_conv, state_pool, page_table, c_prompt,
           c_sample, w_ada, b_ada, norm_mix, norm_ffn, w_in, sb_bias, conv_w, pool_w, pool_scale,
           sg_w, sg_b, sg_norm, w_branch, w_o, w_ff1, w_ff2, norm_final):
    n_seq, t_len, d = x_prompt.shape
    nb, s_len, _ = x_sample.shape
    depth = w_ada.shape[0]
    n_heads, hd = cache_k.shape[3], cache_k.shape[4]
    bw = n_heads * hd
    n_pages, page = page_table.shape[1], cache_k.shape[2]
    past_len = n_pages * page
    n_groups = sg_w.shape[1]
    gwd = bw // n_groups
    q_scale = 0.5 * hd ** -0.5

    w_in_b = w_in.astype(BF16)
    w_qkv, w_rest = w_in_b[:, :, :3 * bw], w_in_b[:, :, 3 * bw:]
    w_branch_b, w_o_b = w_branch.astype(BF16), w_o.astype(BF16)
    w_ff1_b, w_ff2_b = w_ff1.astype(BF16), w_ff2.astype(BF16)
    pool_w_b = pool_w.astype(BF16)
    norm_mix3, norm_ffn3 = norm_mix[:, None, :], norm_ffn[:, None, :]
    pool_scale3, sg_norm3 = pool_scale[:, None, :], sg_norm[:, None, :]
    norm_final2 = norm_final[None, :]
    chunk = min(t_len, SG_CHUNK)
    assert chunk == SG_CHUNK and sg_w.shape[2] == SG_CHUNK
    sg_bt = jnp.repeat(jnp.swapaxes(sg_b, 1, 2), gwd, axis=2)
    ls = min(s_len, SG_CHUNK)
    assert ls == s_len
    sg_wv = jnp.repeat(jnp.transpose(sg_w[:, :, :ls, :ls], (0, 2, 3, 1)), gwd, axis=3)
    sg_wv = sg_wv.reshape(depth, ls * ls, bw)
    sg_bv = jnp.repeat(jnp.swapaxes(sg_b[:, :, :ls], 1, 2), gwd, axis=2)
    bias_cols = jnp.tile(sb_bias, (1, LANES // n_heads))[:, None, :]

    bp = -(-n_seq // SUBLANES) * SUBLANES
    c_all = jnp.concatenate([c_sample, c_prompt, jnp.zeros((bp - n_seq, d), F32)], axis=0)
    mod_s, mod_p = _adaln(c_all, w_ada, b_ada, nb, bp)
    mod_p = mod_p.reshape(depth, bp, 1, N_MOD * d)

    xp = x_prompt.reshape(n_seq * t_len, d)
    xs = x_sample.reshape(nb, s_len * d)
    ck = cache_k.reshape(depth, cache_k.shape[1], page, bw)
    cv = cache_v.reshape(depth, cache_v.shape[1], page, bw)
    sconv = state_conv.reshape(depth, nb, -1)
    spool = state_pool.reshape(depth, nb, -1)

    def prompt_group(want):
        tm = _row_tile(t_len, want)
        return _Group("prompt", n_seq * t_len // tm, tm, d, t_len // tm)

    sample_group = _Group("sample", s_len, nb, d)
    tn_rest = 1024
    tf = 512

    outs = {k: [] for k in ("kp", "vp", "cp", "pp", "ks", "vs", "cs", "ps", "sg")}
    for l in range(depth):
        last = l == depth - 1
        gq = prompt_group(512)
        q, k, v = _qkv_proj(gq, l, xp, mod_p, norm_mix3, w_qkv, bw, q_scale)
        rest = _rest_proj(prompt_group(1024), l, xp, mod_p, norm_mix3, w_rest, tn_rest)
        ya = _attn_prompt(q.reshape(n_seq, t_len, bw), k.reshape(n_seq, t_len, bw),
                          v.reshape(n_seq, t_len, bw), sb_bias[l], hd)
        gm = prompt_group(256)
        xp, cst, pst = _mixer_prompt(gm, l, xp, ya.reshape(n_seq * t_len, bw), rest, mod_p,
                                     conv_w, pool_w_b, pool_scale3, sg_w, sg_bt, sg_norm3,
                                     w_branch_b, w_o_b, bw, n_seq)
        xp = _ffn(prompt_group(1024), l, xp, mod_p, norm_ffn3, w_ff1_b, w_ff2_b, norm_final2, last, tf)
        outs["kp"].append(k.reshape(n_seq, t_len, n_heads, hd))
        outs["vp"].append(v.reshape(n_seq, t_len, n_heads, hd))
        outs["cp"].append(cst[:, cst.shape[1] - (CONV_TAPS - 1):])
        outs["pp"].append(pst[:, pst.shape[1] - POOL_PAD:])

        q, k, v = _qkv_proj(sample_group, l, xs, mod_s, norm_mix3, w_qkv, bw, q_scale)
        rest = _rest_proj(sample_group, l, xs, mod_s, norm_mix3, w_rest, tn_rest)
        ya = _attn_decode(l, q.reshape(nb, s_len, bw), k.reshape(nb, s_len, bw),
                          v.reshape(nb, s_len, bw), ck, cv, page_table, bias_cols[l], hd)
        xs, cst, pst, sgv = _mixer_decode(l, xs, ya.reshape(nb, s_len * bw), rest, sconv, spool,
                                          mod_s, conv_w, pool_w_b, pool_scale3, sg_wv, sg_bv,
                                          sg_norm3, w_branch_b, w_o_b, bw, d, s_len, past_len,
                                          min(nb, 32))
        xs = _ffn(sample_group, l, xs, mod_s, norm_ffn3, w_ff1_b, w_ff2_b, norm_final2, last, tf)
        outs["ks"].append(k.reshape(nb, s_len, n_heads, hd))
        outs["vs"].append(v.reshape(nb, s_len, n_heads, hd))
        outs["cs"].append(cst.reshape(nb, CONV_TAPS - 1, bw))
        outs["ps"].append(pst.reshape(nb, POOL_PAD, bw))
        outs["sg"].append(sgv.reshape(nb, s_len, bw))

    st = {k: jnp.stack(v) for k, v in outs.items()}
    return (xp.reshape(n_seq, t_len, d), xs.reshape(nb, s_len, d), st["kp"], st["vp"], st["cp"],
            st["pp"], st["ks"], st["vs"], st["cs"], st["ps"], st["sg"])
```

```python
import functools

import jax
import jax.numpy as jnp
from jax import lax
from jax.experimental import pallas as pl
from jax.experimental.pallas import tpu as pltpu

F32 = jnp.float32
BF16 = jnp.bfloat16

NORM_EPS = 1e-6
CONV_TAPS = 3
POOL_WINDOWS = (2, 4, 8, 16)
POOL_PAD = max(POOL_WINDOWS) - 1
SG_CHUNK = 128
N_MOD = 6
N_BRANCH = 4

LANES = 128
SUBLANES = 8
KEY_BLOCK = 128
Q_TILE = 256
VMEM_LIMIT_BYTES = 56 * 1024 * 1024


def _params(n_axes):
    return pltpu.CompilerParams(
        dimension_semantics=("arbitrary",) * n_axes, vmem_limit_bytes=VMEM_LIMIT_BYTES)


def _dot(a, b):
    return jnp.dot(a, b, preferred_element_type=F32)


def _dot_nt(a, b):
    return lax.dot_general(a, b, (((1,), (1,)), ((), ())), preferred_element_type=F32)


def _sigmoid(x):
    return 1.0 / (1.0 + jnp.exp(-x))


def _rms(x, g):
    return x * lax.rsqrt(jnp.mean(x * x, axis=-1, keepdims=True) + NORM_EPS) * g


def _norm_mod(x, g, sc, sh):
    return _rms(x, g) * (1.0 + sc) + sh


def _adaln_kernel(c_ref, w_ref, b_ref, os_ref, op_ref):
    c = c_ref[...]
    a = (c * _sigmoid(c)).astype(BF16)
    r = _dot(a, w_ref[...].astype(BF16)) + b_ref[...]
    ns = os_ref.shape[0]
    os_ref[...] = r[:ns]
    op_ref[...] = r[ns:]


def _adaln(c_all, w_ada, b_ada, n_sample, n_prompt_pad):
    depth, d, n = w_ada.shape
    tn = 1024
    rows = c_all.shape[0]
    return pl.pallas_call(
        _adaln_kernel,
        grid=(depth, n // tn),
        in_specs=[
            pl.BlockSpec((rows, d), lambda l, j: (0, 0)),
            pl.BlockSpec((None, d, tn), lambda l, j: (l, 0, j)),
            pl.BlockSpec((None, 1, tn), lambda l, j: (l, 0, j)),
        ],
        out_specs=[
            pl.BlockSpec((None, n_sample, tn), lambda l, j: (l, 0, j)),
            pl.BlockSpec((None, n_prompt_pad, tn), lambda l, j: (l, 0, j)),
        ],
        out_shape=[
            jax.ShapeDtypeStruct((depth, n_sample, n), F32),
            jax.ShapeDtypeStruct((depth, n_prompt_pad, n), F32),
        ],
        compiler_params=_params(2),
        name="adaln",
    )(c_all, w_ada, b_ada.reshape(depth, 1, n))


class _Group:
    def __init__(self, kind, n_tiles, tm, d, tiles_per_seq=None):
        self.kind, self.n_tiles, self.tm, self.d, self.tps = kind, n_tiles, tm, d, tiles_per_seq

    def x_spec(self, width, col_blocks=1, col=0):
        if self.kind == "prompt":
            return pl.BlockSpec((self.tm, width), lambda i, *_: (i, col))
        return pl.BlockSpec((self.tm, width), lambda i, *_: (0, i * col_blocks + col))

    def x_spec_j(self, width, col_blocks):
        if self.kind == "prompt":
            return pl.BlockSpec((self.tm, width), lambda i, j: (i, j))
        return pl.BlockSpec((self.tm, width), lambda i, j: (0, i * col_blocks + j))

    def mod_spec(self, layer, chunk):
        if self.kind == "prompt":
            tps = self.tps
            return pl.BlockSpec((None, None, 1, self.d), lambda i, *_: (layer, i // tps, 0, chunk))
        return pl.BlockSpec((None, self.tm, self.d), lambda i, *_: (layer, 0, chunk))


def _const_spec(shape):
    nd = len(shape)
    return pl.BlockSpec(shape, lambda *_: (0,) * nd)


def _layer_spec(shape, layer):
    nd = len(shape)
    return pl.BlockSpec((None,) + tuple(shape), lambda *_: (layer,) + (0,) * nd)


def _qkv_kernel(x_ref, sh_ref, sc_ref, g_ref, w_ref, q_ref, k_ref, v_ref, *, q_scale):
    h = _norm_mod(x_ref[...], g_ref[...], sc_ref[...], sh_ref[...]).astype(BF16)
    p = _dot(h, w_ref[...])
    bw = q_ref.shape[1]
    q_ref[...] = (p[:, :bw] * q_scale).astype(q_ref.dtype)
    k_ref[...] = p[:, bw:2 * bw]
    v_ref[...] = p[:, 2 * bw:3 * bw]


def _qkv_proj(grp, layer, x, mod, norm_w, w_qkv, bw, q_scale):
    d = grp.d
    rows, cols = x.shape
    out_cols = cols // d * bw
    o_spec = grp.x_spec(bw)
    return pl.pallas_call(
        functools.partial(_qkv_kernel, q_scale=q_scale),
        grid=(grp.n_tiles,),
        in_specs=[
            grp.x_spec(d),
            grp.mod_spec(layer, 0),
            grp.mod_spec(layer, 1),
            _layer_spec((1, d), layer),
            _layer_spec((d, 3 * bw), layer),
        ],
        out_specs=[o_spec, o_spec, o_spec],
        out_shape=[
            jax.ShapeDtypeStruct((rows, out_cols), BF16),
            jax.ShapeDtypeStruct((rows, out_cols), F32),
            jax.ShapeDtypeStruct((rows, out_cols), F32),
        ],
        compiler_params=_params(1),
        name="qkv_proj",
    )(x, mod, mod, norm_w, w_qkv)


def _rest_kernel(x_ref, sh_ref, sc_ref, g_ref, w_ref, o_ref, h_scr):
    @pl.when(pl.program_id(1) == 0)
    def _():
        h_scr[...] = _norm_mod(x_ref[...], g_ref[...], sc_ref[...], sh_ref[...]).astype(BF16)

    o_ref[...] = _dot(h_scr[...], w_ref[...])


def _rest_proj(grp, layer, x, mod, norm_w, w_rest, tn):
    d = grp.d
    rows, cols = x.shape
    n_rest = w_rest.shape[-1]
    nj = n_rest // tn
    return pl.pallas_call(
        _rest_kernel,
        grid=(grp.n_tiles, nj),
        in_specs=[
            grp.x_spec(d),
            grp.mod_spec(layer, 0),
            grp.mod_spec(layer, 1),
            _layer_spec((1, d), layer),
            pl.BlockSpec((None, d, tn), lambda i, j: (layer, 0, j)),
        ],
        out_specs=grp.x_spec_j(tn, nj),
        out_shape=jax.ShapeDtypeStruct((rows, cols // d * n_rest), F32),
        scratch_shapes=[pltpu.VMEM((grp.tm, d), BF16)],
        compiler_params=_params(2),
        name="rest_proj",
    )(x, mod, mod, norm_w, w_rest)


def _suffix_prod_sublanes(x):
    sub = lax.broadcasted_iota(jnp.int32, x.shape, 0)
    y = x
    for k in (1, 2, 4):
        shifted = pltpu.roll(y, SUBLANES - k, axis=0)
        y = y * jnp.where(sub + k < SUBLANES, shifted, 1.0)
    return y


def _shift_up_sublanes(y):
    sub = lax.broadcasted_iota(jnp.int32, y.shape, 0)
    return jnp.where(sub + 1 < SUBLANES, pltpu.roll(y, SUBLANES - 1, axis=0), 1.0)


def _sb_tile_permuted(s_tile, bias_half, carry, diag_offset):
    n = s_tile.shape[1]
    g = 0.5 * jnp.tanh(s_tile + bias_half)
    beta = 0.5 + g
    rest = 0.5 - g
    if diag_offset is not None:
        r = lax.broadcasted_iota(jnp.int32, s_tile.shape, 0)
        q_idx = lax.broadcasted_iota(jnp.int32, s_tile.shape, 1)
        k_idx = (r & 7) * 16 + (r >> 3) + diag_offset
        valid = k_idx < q_idx
        beta = jnp.where(valid, beta, 0.0)
        rest = jnp.where(valid, rest, 1.0)
    nv = KEY_BLOCK // SUBLANES
    rs = [rest[v * SUBLANES:(v + 1) * SUBLANES] for v in range(nv)]
    later = [None] * nv
    acc = None
    for v in range(nv - 1, -1, -1):
        later[v] = acc
        acc = rs[v] if acc is None else acc * rs[v]
    incl = _suffix_prod_sublanes(acc)
    base = _shift_up_sublanes(incl) * carry
    ws = []
    for v in range(nv):
        p = base if later[v] is None else later[v] * base
        ws.append(beta[v * SUBLANES:(v + 1) * SUBLANES] * p)
    w = jnp.concatenate(ws, axis=0).astype(BF16)
    return w, carry * incl[0:1]


def _attn_prompt_kernel(bias_ref, q_ref, k_ref, v_ref, o_ref, kp_scr, vt_scr, qt_scr, s_a, s_b, w_a, w_b,
                        acc_scr, *, hd):
    hp = pl.program_id(1)
    i = pl.program_id(2)
    tq = q_ref.shape[0]
    n_chunks = k_ref.shape[0] // tq
    halves = tq // KEY_BLOCK

    @pl.when(i == 0)
    def _():
        r = lax.broadcasted_iota(jnp.int32, (KEY_BLOCK, KEY_BLOCK), 0)
        j = lax.broadcasted_iota(jnp.int32, (KEY_BLOCK, KEY_BLOCK), 1)
        perm = jnp.where(j == (r & 7) * 16 + (r >> 3), 1.0, 0.0).astype(BF16)

        def body(c, carry):
            for hf in range(halves):
                rows = pl.ds(pl.multiple_of(c * tq + hf * KEY_BLOCK, KEY_BLOCK), KEY_BLOCK)
                kp = _dot(perm, k_ref[rows, :].astype(BF16))
                vp = _dot(perm, v_ref[rows, :].astype(BF16))
                kp_scr[c, hf * KEY_BLOCK:(hf + 1) * KEY_BLOCK, :] = kp.astype(BF16)
                vt_scr[c, :, hf * KEY_BLOCK:(hf + 1) * KEY_BLOCK] = vp.T.astype(BF16)
            return carry

        lax.fori_loop(0, n_chunks, body, 0)

    q = q_ref[...].astype(F32)
    lane = lax.broadcasted_iota(jnp.int32, q.shape, 1)
    qt_scr[0] = jnp.where(lane < hd, q, 0.0).T.astype(BF16)
    qt_scr[1] = jnp.where(lane >= hd, q, 0.0).T.astype(BF16)
    bias_half = (0.5 * bias_ref[2 * hp], 0.5 * bias_ref[2 * hp + 1])
    acc_scr[...] = jnp.zeros_like(acc_scr)

    def scores(k, s_buf):
        k_blk = kp_scr[jnp.maximum(i - k, 0)]
        for h in range(2):
            s_buf[h] = _dot(k_blk, qt_scr[h])

    def scan(s_buf, w_buf, diag, carry):
        out = []
        for h in range(2):
            cr = carry[h]
            for hf in range(halves - 1, -1, -1):
                rows = slice(hf * KEY_BLOCK, (hf + 1) * KEY_BLOCK)
                w, cr = _sb_tile_permuted(s_buf[h, rows, :], bias_half[h], cr,
                                          hf * KEY_BLOCK if diag else None)
                w_buf[h, rows, :] = w
            out.append(cr)
        return tuple(out)

    def values(k, w_buf):
        vt_blk = vt_scr[i - k]
        for h in range(2):
            acc_scr[h] += _dot(vt_blk, w_buf[h])

    n = i + 1
    ones = jnp.ones((1, tq), F32)
    scores(0, s_a)
    scores(1, s_b)
    carry = scan(s_a, w_a, True, (ones, ones))

    def pair(m, cr):
        k = 2 * m
        scores(k, s_a)
        cr = scan(s_b, w_b, False, cr)
        values(k - 2, w_a)
        scores(k + 1, s_b)
        cr = scan(s_a, w_a, False, cr)
        values(k - 1, w_b)
        return cr

    carry = lax.fori_loop(1, (n - 1) // 2 + 1, pair, carry)

    @pl.when(n % 2 == 0)
    def _():
        scan(s_b, w_b, False, carry)
        values(n - 2, w_a)
        values(n - 1, w_b)

    @pl.when(n % 2 == 1)
    def _():
        values(n - 1, w_a)

    row = lax.broadcasted_iota(jnp.int32, (2 * hd, tq), 0)
    res = jnp.where(row < hd, acc_scr[0], acc_scr[1])
    o_ref[...] = res.T.astype(o_ref.dtype)


def _attn_prompt(q, k, v, bias, hd):
    b, t, c = q.shape
    assert 2 * hd == LANES and t % Q_TILE == 0
    n_chunks = t // Q_TILE
    qo_spec = pl.BlockSpec((None, Q_TILE, LANES), lambda bi, hp, i: (bi, i, hp))
    kv_spec = pl.BlockSpec((None, t, LANES), lambda bi, hp, i: (bi, 0, hp))
    return pl.pallas_call(
        functools.partial(_attn_prompt_kernel, hd=hd),
        grid=(b, c // LANES, n_chunks),
        in_specs=[pl.BlockSpec(memory_space=pltpu.SMEM), qo_spec, kv_spec, kv_spec],
        out_specs=qo_spec,
        out_shape=jax.ShapeDtypeStruct((b, t, c), BF16),
        scratch_shapes=[
            pltpu.VMEM((n_chunks, Q_TILE, LANES), BF16),
            pltpu.VMEM((n_chunks, LANES, Q_TILE), BF16),
            pltpu.VMEM((2, LANES, Q_TILE), BF16),
            pltpu.VMEM((2, Q_TILE, Q_TILE), F32),
            pltpu.VMEM((2, Q_TILE, Q_TILE), F32),
            pltpu.VMEM((2, Q_TILE, Q_TILE), BF16),
            pltpu.VMEM((2, Q_TILE, Q_TILE), BF16),
            pltpu.VMEM((2, LANES, Q_TILE), F32),
        ],
        compiler_params=_params(3),
        name="attn_prompt",
    )(bias, q, k, v)


def _suffix_prod_lanes(x):
    n = x.shape[1]
    lane = lax.broadcasted_iota(jnp.int32, x.shape, 1)
    y = x
    k = 1
    while k < n:
        shifted = pltpu.roll(y, n - k, axis=1)
        y = y * jnp.where(lane + k < n, shifted, 1.0)
        k *= 2
    return y


def _attn_decode_kernel(pt_ref, bias_ref, q_ref, kn_ref, vn_ref, *rest, hd, n_pages):
    del pt_ref
    k_pages = rest[:n_pages]
    v_pages = rest[n_pages:2 * n_pages]
    o_ref = rest[2 * n_pages]
    s_len, c = q_ref.shape
    n_heads = c // hd
    nrow = s_len * n_heads

    head_row = lax.broadcasted_iota(jnp.int32, (n_heads, c), 0)
    lane = lax.broadcasted_iota(jnp.int32, (n_heads, c), 1)
    head_mask = (lane >= head_row * hd) & (lane < (head_row + 1) * hd)

    q = q_ref[...].astype(F32)
    q_rows = jnp.concatenate(
        [jnp.where(head_mask, jnp.broadcast_to(q[t:t + 1], (n_heads, c)), 0.0) for t in range(s_len)],
        axis=0).astype(BF16)
    bias_half = 0.5 * bias_ref[...]

    pad = jnp.zeros((KEY_BLOCK - s_len, c), F32)
    k_new = jnp.concatenate([kn_ref[...], pad], axis=0).astype(BF16)
    v_new = jnp.concatenate([vn_ref[...], pad], axis=0).astype(BF16)
    row = lax.broadcasted_iota(jnp.int32, (nrow, KEY_BLOCK), 0)
    key = lax.broadcasted_iota(jnp.int32, (nrow, KEY_BLOCK), 1)
    own_valid = key * n_heads + n_heads <= row

    pages = range(n_pages - 1, -1, -1)
    g = [0.5 * jnp.tanh(_dot_nt(q_rows, k_new) + bias_half)]
    g += [0.5 * jnp.tanh(_dot(q_rows, k_pages[p][...].reshape(c, KEY_BLOCK).astype(BF16)) + bias_half)
          for p in pages]
    beta = [jnp.where(own_valid, 0.5 + g[0], 0.0)] + [0.5 + gj for gj in g[1:]]
    rest_all = jnp.concatenate([jnp.where(own_valid, 0.5 - g[0], 1.0)] + [0.5 - gj for gj in g[1:]], axis=0)
    incl = _suffix_prod_lanes(rest_all)
    lane = lax.broadcasted_iota(jnp.int32, incl.shape, 1)
    later = jnp.where(lane + 1 < KEY_BLOCK, pltpu.roll(incl, KEY_BLOCK - 1, axis=1), 1.0)
    total = jnp.broadcast_to(incl[:, 0:1], incl.shape)
    carry = None
    acc = None
    for j in range(n_pages + 1):
        rows = slice(j * nrow, (j + 1) * nrow)
        p_later = later[rows] if carry is None else later[rows] * carry
        w = (beta[j] * p_later).astype(BF16)
        carry = total[rows] if carry is None else carry * total[rows]
        if j == 0:
            acc = _dot(w, v_new)
        else:
            acc = acc + _dot_nt(w, v_pages[n_pages - j][...].reshape(c, KEY_BLOCK).astype(BF16))

    outs = [jnp.sum(jnp.where(head_mask, acc[t * n_heads:(t + 1) * n_heads], 0.0), axis=0, keepdims=True)
            for t in range(s_len)]
    o_ref[...] = jnp.concatenate(outs, axis=0)


def _attn_decode(layer, q, k_new, v_new, cache_kt, cache_vt, page_table, bias_rows, hd):
    nb, s_len, c = q.shape
    n_pages = page_table.shape[1]
    n_heads, page = cache_kt.shape[2], cache_kt.shape[4]
    nrow = s_len * n_heads
    assert page == KEY_BLOCK and n_heads == SUBLANES and n_heads * hd == c
    seq_spec = pl.BlockSpec((None, s_len, c), lambda b, pt: (b, 0, 0))

    def page_spec(p):
        return pl.BlockSpec((None, None, n_heads, hd, page), lambda b, pt: (layer, pt[b, p], 0, 0, 0))

    grid_spec = pltpu.PrefetchScalarGridSpec(
        num_scalar_prefetch=1,
        grid=(nb,),
        in_specs=[pl.BlockSpec((None, nrow, KEY_BLOCK), lambda b, pt: (layer, 0, 0)),
                  seq_spec, seq_spec, seq_spec]
        + [page_spec(p) for p in range(n_pages)] + [page_spec(p) for p in range(n_pages)],
        out_specs=seq_spec,
    )
    return pl.pallas_call(
        functools.partial(_attn_decode_kernel, hd=hd, n_pages=n_pages),
        grid_spec=grid_spec,
        out_shape=jax.ShapeDtypeStruct((nb, s_len, c), F32),
        compiler_params=_params(1),
        name="attn_decode",
    )(page_table, bias_rows, q, k_new, v_new, *([cache_kt] * n_pages), *([cache_vt] * n_pages))


def _merge(x, ys, gates, wb_ref, wo_ref, g1):
    merged = None
    for i, (y, gate) in enumerate(zip(ys, gates)):
        term = _sigmoid(gate) * _dot(y.astype(BF16), wb_ref[i])
        merged = term if merged is None else merged + term
    return x + g1 * _dot(merged.astype(BF16), wo_ref[...])


def _pool_count(pos, win):
    return jnp.minimum(pos + 1.0, float(win))


def _mixer_prompt_kernel(x_ref, ya_ref, cv_ref, cvh_ref, pi_ref, pih_ref, su_ref, sv_ref,
                         g0_ref, g1_ref, g2_ref, g3_ref, cw_ref, pw_ref, ps_ref, sw_ref, sb_ref,
                         sn_ref, wb_ref, wo_ref, m1_ref, o_ref, cst_ref, pst_ref, cbuf, pbuf,
                         *, bw, tiles_per_seq):
    tm = x_ref.shape[0]
    first = (pl.program_id(0) % tiles_per_seq) == 0
    hc = cvh_ref.shape[0]
    hp = pih_ref.shape[0]

    cv = cv_ref[...]
    cx = cv[:, 2 * bw:3 * bw] * cv[:, 0:bw]
    cvh = cvh_ref[...]
    cbuf[0:hc, :] = jnp.where(first, 0.0, cvh[:, 2 * bw:3 * bw] * cvh[:, 0:bw])
    cbuf[hc:hc + tm, :] = cx
    cw = cw_ref[...]
    conv = cw[0:1] * cbuf[pl.ds(hc - 2, tm), :] + cw[1:2] * cbuf[pl.ds(hc - 1, tm), :] + cw[2:3] * cx
    y_b = cv[:, bw:2 * bw] * conv
    cst_ref[...] = cx[tm - hc:tm]

    pin = pi_ref[...]
    pbuf[0:hp, :] = jnp.where(first, 0.0, pih_ref[...])
    pbuf[hp:hp + tm, :] = pin
    pst_ref[...] = pin[tm - hp:tm]
    pos = (lax.broadcasted_iota(jnp.int32, (tm, 1), 0)
           + (pl.program_id(0) % tiles_per_seq) * tm).astype(F32)
    gw = bw // len(POOL_WINDOWS)
    y_c = []
    for g, win in enumerate(POOL_WINDOWS):
        lanes = slice(g * gw, (g + 1) * gw)
        wsum = pin[:, lanes]
        for kk in range(1, win):
            wsum = wsum + pbuf[pl.ds(hp - kk, tm), lanes]
        diff = wsum / _pool_count(pos, win) - pin[:, lanes]
        y_c.append(_dot(diff.astype(BF16), pw_ref[g]))
    y_c = jnp.concatenate(y_c, axis=1) * ps_ref[...]

    vn = _rms(sv_ref[...], sn_ref[...])
    n_groups = sw_ref.shape[0] // SG_CHUNK
    gwd = bw // n_groups
    r = lax.broadcasted_iota(jnp.int32, sw_ref.shape, 0) & (SG_CHUNK - 1)
    s_idx = lax.broadcasted_iota(jnp.int32, sw_ref.shape, 1)
    w_tril = jnp.where(s_idx <= r, sw_ref[...], 0.0).astype(BF16)
    lane_grp = lax.broadcasted_iota(jnp.int32, (SG_CHUNK, bw), 1)
    mixed = []
    for n in range(tm // SG_CHUNK):
        full = _dot(w_tril, vn[n * SG_CHUNK:(n + 1) * SG_CHUNK].astype(BF16))
        m = None
        for g in range(n_groups):
            blk = full[g * SG_CHUNK:(g + 1) * SG_CHUNK]
            sel = jnp.where((lane_grp >= g * gwd) & (lane_grp < (g + 1) * gwd), blk, 0.0)
            m = sel if m is None else m + sel
        mixed.append(m + sb_ref[...])
    y_d = su_ref[...] * jnp.concatenate(mixed, axis=0)

    gates = (g0_ref[...], g1_ref[...], g2_ref[...], g3_ref[...])
    o_ref[...] = _merge(x_ref[...], (ya_ref[...], y_b, y_c, y_d), gates, wb_ref, wo_ref, m1_ref[...])


def _mixer_prompt(grp, layer, x, ya, rest, mod, conv_w, pool_w, pool_scale, sg_w, sg_bt, sg_norm,
                  w_branch, w_o, bw, n_seq):
    d, tm, tps = grp.d, grp.tm, grp.tps
    rows = x.shape[0]
    hc, hp = SUBLANES, 2 * SUBLANES
    assert hc >= CONV_TAPS - 1 and hp >= POOL_PAD and tm % SG_CHUNK == 0
    gate_col = 6 * bw // d

    def halo(h, width, col):
        per = tm // h
        return pl.BlockSpec((h, width), lambda i: (jnp.maximum(i * per - 1, 0), col))

    n_groups = sg_w.shape[1]
    return pl.pallas_call(
        functools.partial(_mixer_prompt_kernel, bw=bw, tiles_per_seq=tps),
        grid=(grp.n_tiles,),
        in_specs=[
            grp.x_spec(d), grp.x_spec(bw),
            grp.x_spec(3 * bw, col=0), halo(hc, 3 * bw, 0),
            grp.x_spec(bw, col=3), halo(hp, bw, 3),
            grp.x_spec(bw, col=4), grp.x_spec(bw, col=5),
            grp.x_spec(d, col=gate_col), grp.x_spec(d, col=gate_col + 1),
            grp.x_spec(d, col=gate_col + 2), grp.x_spec(d, col=gate_col + 3),
            _layer_spec((CONV_TAPS, bw), layer),
            _layer_spec(pool_w.shape[1:], layer),
            _layer_spec((1, bw), layer),
            _layer_spec((n_groups * SG_CHUNK, SG_CHUNK), layer),
            _layer_spec((SG_CHUNK, bw), layer),
            _layer_spec((1, bw), layer),
            _layer_spec(w_branch.shape[1:], layer),
            _layer_spec(w_o.shape[1:], layer),
            grp.mod_spec(layer, 2),
        ],
        out_specs=[
            grp.x_spec(d),
            pl.BlockSpec((None, hc, bw), lambda i: (i // tps, 0, 0)),
            pl.BlockSpec((None, hp, bw), lambda i: (i // tps, 0, 0)),
        ],
        out_shape=[
            jax.ShapeDtypeStruct((rows, d), F32),
            jax.ShapeDtypeStruct((n_seq, hc, bw), F32),
            jax.ShapeDtypeStruct((n_seq, hp, bw), F32),
        ],
        scratch_shapes=[pltpu.VMEM((hc + tm, bw), F32), pltpu.VMEM((hp + tm, bw), F32)],
        compiler_params=_params(1),
        name="mixer_prompt",
    )(x, ya, rest, rest, rest, rest, rest, rest, rest, rest, rest, rest,
      conv_w, pool_w, pool_scale, sg_w.reshape(sg_w.shape[0], n_groups * SG_CHUNK, SG_CHUNK),
      sg_bt, sg_norm, w_branch, w_o, mod)


def _mixer_decode_kernel(x_ref, ya_ref, r_ref, cs_ref, ps_ref, cw_ref, pw_ref, psc_ref, swv_ref,
                         sbv_ref, sn_ref, wb_ref, wo_ref, m1_ref,
                         o_ref, cst_ref, pst_ref, sgv_ref, *, bw, d, s_len, pos0):
    n_rest = r_ref.shape[1] // s_len

    def rest(t, off, width):
        return r_ref[:, t * n_rest + off:t * n_rest + off + width]

    cw = cw_ref[...]
    cx = [rest(t, 2 * bw, bw) * rest(t, 0, bw) for t in range(s_len)]
    taps = CONV_TAPS - 1
    xp = [cs_ref[:, i * bw:(i + 1) * bw] for i in range(taps)] + cx
    y_b = []
    for t in range(s_len):
        conv = cw[0:1] * xp[t]
        for i in range(1, CONV_TAPS):
            conv = conv + cw[i:i + 1] * xp[t + i]
        y_b.append(rest(t, bw, bw) * conv)
    for i in range(taps):
        cst_ref[:, i * bw:(i + 1) * bw] = xp[len(xp) - taps + i]

    pin = [rest(t, 3 * bw, bw) for t in range(s_len)]
    pp = [ps_ref[:, i * bw:(i + 1) * bw] for i in range(POOL_PAD)] + pin
    for i in range(POOL_PAD):
        pst_ref[:, i * bw:(i + 1) * bw] = pp[len(pp) - POOL_PAD + i]
    gw = bw // len(POOL_WINDOWS)
    diffs = [[] for _ in POOL_WINDOWS]
    for t in range(s_len):
        for g, win in enumerate(POOL_WINDOWS):
            lanes = slice(g * gw, (g + 1) * gw)
            wsum = pp[POOL_PAD + t][:, lanes]
            for kk in range(1, win):
                wsum = wsum + pp[POOL_PAD + t - kk][:, lanes]
            cnt = float(min(pos0 + t + 1, win))
            diffs[g].append(wsum / cnt - pin[t][:, lanes])
    y_c_groups = [_dot(jnp.concatenate(diffs[g], axis=0).astype(BF16), pw_ref[g])
                  for g in range(len(POOL_WINDOWS))]
    y_c_all = jnp.concatenate(y_c_groups, axis=1) * psc_ref[...]

    tb = x_ref.shape[0]
    vn = [_rms(rest(t, 5 * bw, bw), sn_ref[...]) for t in range(s_len)]
    y_d = []
    for t in range(s_len):
        sgv_ref[:, t * bw:(t + 1) * bw] = vn[t]
        mixed = sbv_ref[t:t + 1]
        for s in range(t + 1):
            mixed = mixed + swv_ref[t * s_len + s:t * s_len + s + 1] * vn[s]
        y_d.append(rest(t, 4 * bw, bw) * mixed)

    cat = lambda parts: jnp.concatenate(parts, axis=0)
    x_all = cat([x_ref[:, t * d:(t + 1) * d] for t in range(s_len)])
    ya_all = cat([ya_ref[:, t * bw:(t + 1) * bw] for t in range(s_len)])
    gates = [cat([rest(t, 6 * bw + i * d, d) for t in range(s_len)]) for i in range(N_BRANCH)]
    g1 = cat([m1_ref[...]] * s_len)
    out = _merge(x_all, (ya_all, cat(y_b), y_c_all, cat(y_d)), gates, wb_ref, wo_ref, g1)
    for t in range(s_len):
        o_ref[:, t * d:(t + 1) * d] = out[t * tb:(t + 1) * tb]


def _mixer_decode(layer, x, ya, rest, state_conv, state_pool, mod, conv_w, pool_w, pool_scale,
                  sg_wv, sg_bv, sg_norm, w_branch, w_o, bw, d, s_len, pos0, tb):
    nb = x.shape[0]

    def row_spec(width):
        return pl.BlockSpec((tb, width), lambda i: (i, 0))

    def state_spec(width):
        return pl.BlockSpec((None, tb, width), lambda i: (layer, i, 0))

    return pl.pallas_call(
        functools.partial(_mixer_decode_kernel, bw=bw, d=d, s_len=s_len, pos0=pos0),
        grid=(nb // tb,),
        in_specs=[
            row_spec(s_len * d), row_spec(s_len * bw), row_spec(rest.shape[1]),
            state_spec(state_conv.shape[2]), state_spec(state_pool.shape[2]),
            _layer_spec((CONV_TAPS, bw), layer),
            _layer_spec(pool_w.shape[1:], layer),
            _layer_spec((1, bw), layer),
            _layer_spec(sg_wv.shape[1:], layer),
            _layer_spec(sg_bv.shape[1:], layer),
            _layer_spec((1, bw), layer),
            _layer_spec(w_branch.shape[1:], layer),
            _layer_spec(w_o.shape[1:], layer),
            pl.BlockSpec((None, tb, d), lambda i: (layer, i, 2)),
        ],
        out_specs=[row_spec(s_len * d), row_spec(state_conv.shape[2]), row_spec(state_pool.shape[2]),
                   row_spec(s_len * bw)],
        out_shape=[
            jax.ShapeDtypeStruct((nb, s_len * d), F32),
            jax.ShapeDtypeStruct((nb, state_conv.shape[2]), F32),
            jax.ShapeDtypeStruct((nb, state_pool.shape[2]), F32),
            jax.ShapeDtypeStruct((nb, s_len * bw), F32),
        ],
        compiler_params=_params(1),
        name="mixer_decode",
    )(x, ya, rest, state_conv, state_pool, conv_w, pool_w, pool_scale, sg_wv, sg_bv, sg_norm,
      w_branch, w_o, mod)


def _ffn_kernel(x_ref, sh_ref, sc_ref, gt_ref, g_ref, w1_ref, w2_ref, fw_ref, o_ref, h_scr, acc_scr,
                *, final_norm):
    j = pl.program_id(1)

    @pl.when(j == 0)
    def _():
        h_scr[...] = _norm_mod(x_ref[...], g_ref[...], sc_ref[...], sh_ref[...]).astype(BF16)
        acc_scr[...] = jnp.zeros_like(acc_scr)

    a = jnp.maximum(_dot(h_scr[...], w1_ref[...]), 0.0)
    acc_scr[...] += _dot((a * a).astype(BF16), w2_ref[...])

    @pl.when(j == pl.num_programs(1) - 1)
    def _():
        y = x_ref[...] + gt_ref[...] * acc_scr[...]
        if final_norm:
            y = _rms(y, fw_ref[...])
        o_ref[...] = y


def _ffn(grp, layer, x, mod, norm_w, w1, w2, final_w, final_norm, tf):
    d = grp.d
    dff = w1.shape[-1]
    return pl.pallas_call(
        functools.partial(_ffn_kernel, final_norm=final_norm),
        grid=(grp.n_tiles, dff // tf),
        in_specs=[
            grp.x_spec(d),
            grp.mod_spec(layer, 3), grp.mod_spec(layer, 4), grp.mod_spec(layer, 5),
            _layer_spec((1, d), layer),
            pl.BlockSpec((None, d, tf), lambda i, j: (layer, 0, j)),
            pl.BlockSpec((None, tf, d), lambda i, j: (layer, j, 0)),
            _const_spec((1, d)),
        ],
        out_specs=grp.x_spec(d),
        out_shape=jax.ShapeDtypeStruct(x.shape, F32),
        scratch_shapes=[pltpu.VMEM((grp.tm, d), BF16), pltpu.VMEM((grp.tm, d), F32)],
        compiler_params=_params(2),
        name="ffn",
    )(x, mod, mod, mod, norm_w, w1, w2, final_w)


def _row_tile(t, want):
    tm = min(t, want)
    assert t % tm == 0
    return tm


def kernel(x_prompt, x_sample, cache_k, cache_v, state_conv, state_pool, page_table, c_prompt,
           c_sample, w_ada, b_ada, norm_mix, norm_ffn, w_in, sb_bias, conv_w, pool_w, pool_scale,
           sg_w, sg_b, sg_norm, w_branch, w_o, w_ff1, w_ff2, norm_final):
    n_seq, t_len, d = x_prompt.shape
    nb, s_len, _ = x_sample.shape
    depth = w_ada.shape[0]
    n_heads, hd = cache_k.shape[3], cache_k.shape[4]
    bw = n_heads * hd
    n_pages, page = page_table.shape[1], cache_k.shape[2]
    past_len = n_pages * page
    n_groups = sg_w.shape[1]
    gwd = bw // n_groups
    q_scale = 0.5 * hd ** -0.5

    w_in_b = w_in.astype(BF16)
    w_qkv, w_rest = w_in_b[:, :, :3 * bw], w_in_b[:, :, 3 * bw:]
    w_branch_b, w_o_b = w_branch.astype(BF16), w_o.astype(BF16)
    w_ff1_b, w_ff2_b = w_ff1.astype(BF16), w_ff2.astype(BF16)
    pool_w_b = pool_w.astype(BF16)
    norm_mix3, norm_ffn3 = norm_mix[:, None, :], norm_ffn[:, None, :]
    pool_scale3, sg_norm3 = pool_scale[:, None, :], sg_norm[:, None, :]
    norm_final2 = norm_final[None, :]
    chunk = min(t_len, SG_CHUNK)
    assert chunk == SG_CHUNK and sg_w.shape[2] == SG_CHUNK
    sg_bt = jnp.repeat(jnp.swapaxes(sg_b, 1, 2), gwd, axis=2)
    ls = min(s_len, SG_CHUNK)
    assert ls == s_len
    sg_wv = jnp.repeat(jnp.transpose(sg_w[:, :, :ls, :ls], (0, 2, 3, 1)), gwd, axis=3)
    sg_wv = sg_wv.reshape(depth, ls * ls, bw)
    sg_bv = jnp.repeat(jnp.swapaxes(sg_b[:, :, :ls], 1, 2), gwd, axis=2)
    bias_rows = jnp.broadcast_to(jnp.tile(sb_bias, (1, s_len))[:, :, None],
                                 (depth, s_len * n_heads, KEY_BLOCK))

    bp = -(-n_seq // SUBLANES) * SUBLANES
    c_all = jnp.concatenate([c_sample, c_prompt, jnp.zeros((bp - n_seq, d), F32)], axis=0)
    mod_s, mod_p = _adaln(c_all, w_ada, b_ada, nb, bp)
    mod_p = mod_p.reshape(depth, bp, 1, N_MOD * d)

    xp = x_prompt.reshape(n_seq * t_len, d)
    xs = x_sample.reshape(nb, s_len * d)
    ck = jnp.transpose(cache_k, (0, 1, 3, 4, 2))
    cv = jnp.transpose(cache_v, (0, 1, 3, 4, 2))
    sconv = state_conv.reshape(depth, nb, -1)
    spool = state_pool.reshape(depth, nb, -1)

    def prompt_group(want):
        tm = _row_tile(t_len, want)
        return _Group("prompt", n_seq * t_len // tm, tm, d, t_len // tm)

    sample_group = _Group("sample", s_len, nb, d)
    tn_rest = 1024
    tf = 512

    outs = {k: [] for k in ("kp", "vp", "cp", "pp", "ks", "vs", "cs", "ps", "sg")}
    for l in range(depth):
        last = l == depth - 1
        gq = prompt_group(512)
        q, k, v = _qkv_proj(gq, l, xp, mod_p, norm_mix3, w_qkv, bw, q_scale)
        rest = _rest_proj(prompt_group(1024), l, xp, mod_p, norm_mix3, w_rest, tn_rest)
        ya = _attn_prompt(q.reshape(n_seq, t_len, bw), k.reshape(n_seq, t_len, bw),
                          v.reshape(n_seq, t_len, bw), sb_bias[l], hd)
        gm = prompt_group(256)
        xp, cst, pst = _mixer_prompt(gm, l, xp, ya.reshape(n_seq * t_len, bw), rest, mod_p,
                                     conv_w, pool_w_b, pool_scale3, sg_w, sg_bt, sg_norm3,
                                     w_branch_b, w_o_b, bw, n_seq)
        xp = _ffn(prompt_group(1024), l, xp, mod_p, norm_ffn3, w_ff1_b, w_ff2_b, norm_final2, last, tf)
        outs["kp"].append(k.reshape(n_seq, t_len, n_heads, hd))
        outs["vp"].append(v.reshape(n_seq, t_len, n_heads, hd))
        outs["cp"].append(cst[:, cst.shape[1] - (CONV_TAPS - 1):])
        outs["pp"].append(pst[:, pst.shape[1] - POOL_PAD:])

        q, k, v = _qkv_proj(sample_group, l, xs, mod_s, norm_mix3, w_qkv, bw, q_scale)
        rest = _rest_proj(sample_group, l, xs, mod_s, norm_mix3, w_rest, tn_rest)
        ya = _attn_decode(l, q.reshape(nb, s_len, bw), k.reshape(nb, s_len, bw),
                          v.reshape(nb, s_len, bw), ck, cv, page_table, bias_rows, hd)
        xs, cst, pst, sgv = _mixer_decode(l, xs, ya.reshape(nb, s_len * bw), rest, sconv, spool,
                                          mod_s, conv_w, pool_w_b, pool_scale3, sg_wv, sg_bv,
                                          sg_norm3, w_branch_b, w_o_b, bw, d, s_len, past_len,
                                          min(nb, 32))
        xs = _ffn(sample_group, l, xs, mod_s, norm_ffn3, w_ff1_b, w_ff2_b, norm_final2, last, tf)
        outs["ks"].append(k.reshape(nb, s_len, n_heads, hd))
        outs["vs"].append(v.reshape(nb, s_len, n_heads, hd))
        outs["cs"].append(cst.reshape(nb, CONV_TAPS - 1, bw))
        outs["ps"].append(pst.reshape(nb, POOL_PAD, bw))
        outs["sg"].append(sgv.reshape(nb, s_len, bw))

    st = {k: jnp.stack(v) for k, v in outs.items()}
    return (xp.reshape(n_seq, t_len, d), xs.reshape(nb, s_len, d), st["kp"], st["vp"], st["cp"],
            st["pp"], st["ks"], st["vs"], st["cs"], st["ps"], st["sg"])
```

```python
import functools

import jax
import jax.numpy as jnp
from jax import lax
from jax.experimental import pallas as pl
from jax.experimental.pallas import tpu as pltpu

F32 = jnp.float32
BF16 = jnp.bfloat16

NORM_EPS = 1e-6
CONV_TAPS = 3
POOL_WINDOWS = (2, 4, 8, 16)
POOL_PAD = max(POOL_WINDOWS) - 1
SG_CHUNK = 128
N_MOD = 6
N_BRANCH = 4

LANES = 128
SUBLANES = 8
KEY_BLOCK = 128
Q_TILE = 256
VMEM_LIMIT_BYTES = 56 * 1024 * 1024


def _params(n_axes):
    return pltpu.CompilerParams(
        dimension_semantics=("arbitrary",) * n_axes, vmem_limit_bytes=VMEM_LIMIT_BYTES)


def _dot(a, b):
    return jnp.dot(a, b, preferred_element_type=F32)


def _dot_nt(a, b):
    return lax.dot_general(a, b, (((1,), (1,)), ((), ())), preferred_element_type=F32)


def _sigmoid(x):
    return 1.0 / (1.0 + jnp.exp(-x))


def _rms(x, g):
    return x * lax.rsqrt(jnp.mean(x * x, axis=-1, keepdims=True) + NORM_EPS) * g


def _norm_mod(x, g, sc, sh):
    return _rms(x, g) * (1.0 + sc) + sh


def _adaln_kernel(c_ref, w_ref, b_ref, os_ref, op_ref):
    c = c_ref[...]
    a = (c * _sigmoid(c)).astype(BF16)
    r = _dot(a, w_ref[...].astype(BF16)) + b_ref[...]
    ns = os_ref.shape[0]
    os_ref[...] = r[:ns]
    op_ref[...] = r[ns:]


def _adaln(c_all, w_ada, b_ada, n_sample, n_prompt_pad):
    depth, d, n = w_ada.shape
    tn = 1024
    rows = c_all.shape[0]
    return pl.pallas_call(
        _adaln_kernel,
        grid=(depth, n // tn),
        in_specs=[
            pl.BlockSpec((rows, d), lambda l, j: (0, 0)),
            pl.BlockSpec((None, d, tn), lambda l, j: (l, 0, j)),
            pl.BlockSpec((None, 1, tn), lambda l, j: (l, 0, j)),
        ],
        out_specs=[
            pl.BlockSpec((None, n_sample, tn), lambda l, j: (l, 0, j)),
            pl.BlockSpec((None, n_prompt_pad, tn), lambda l, j: (l, 0, j)),
        ],
        out_shape=[
            jax.ShapeDtypeStruct((depth, n_sample, n), F32),
            jax.ShapeDtypeStruct((depth, n_prompt_pad, n), F32),
        ],
        compiler_params=_params(2),
        name="adaln",
    )(c_all, w_ada, b_ada.reshape(depth, 1, n))


class _Group:
    def __init__(self, kind, n_tiles, tm, d, tiles_per_seq=None):
        self.kind, self.n_tiles, self.tm, self.d, self.tps = kind, n_tiles, tm, d, tiles_per_seq

    def x_spec(self, width, col_blocks=1, col=0):
        if self.kind == "prompt":
            return pl.BlockSpec((self.tm, width), lambda i, *_: (i, col))
        return pl.BlockSpec((self.tm, width), lambda i, *_: (0, i * col_blocks + col))

    def x_spec_j(self, width, col_blocks):
        if self.kind == "prompt":
            return pl.BlockSpec((self.tm, width), lambda i, j: (i, j))
        return pl.BlockSpec((self.tm, width), lambda i, j: (0, i * col_blocks + j))

    def mod_spec(self, layer, chunk):
        if self.kind == "prompt":
            tps = self.tps
            return pl.BlockSpec((None, None, 1, self.d), lambda i, *_: (layer, i // tps, 0, chunk))
        return pl.BlockSpec((None, self.tm, self.d), lambda i, *_: (layer, 0, chunk))


def _const_spec(shape):
    nd = len(shape)
    return pl.BlockSpec(shape, lambda *_: (0,) * nd)


def _layer_spec(shape, layer):
    nd = len(shape)
    return pl.BlockSpec((None,) + tuple(shape), lambda *_: (layer,) + (0,) * nd)


def _qkv_kernel(x_ref, sh_ref, sc_ref, g_ref, w_ref, *rest, q_scale):
    q_ref, k_ref, v_ref, kt_ref, vt_ref = rest[-5:]
    h = _norm_mod(x_ref[...], g_ref[...], sc_ref[...], sh_ref[...]).astype(BF16)
    p = _dot(h, w_ref[...])
    bw = q_ref.shape[1]
    k, v = p[:, bw:2 * bw], p[:, 2 * bw:3 * bw]
    q_ref[...] = (p[:, :bw] * q_scale).astype(q_ref.dtype)
    k_ref[...] = k.astype(k_ref.dtype)
    v_ref[...] = v.astype(v_ref.dtype)
    kt_ref[...] = k.T
    vt_ref[...] = v.T


def _qkv_proj(grp, layer, x, mod, norm_w, w_qkv, bw, q_scale, kv_dtype, state_shape, state_kt, state_vt):
    d = grp.d
    rows, cols = x.shape
    out_cols = cols // d * bw
    o_spec = grp.x_spec(bw)
    if grp.kind == "prompt":
        tps = grp.tps
        t_spec = pl.BlockSpec((None, None, bw, grp.tm), lambda i: (layer, i // tps, 0, i % tps))
    else:
        t_spec = pl.BlockSpec((None, None, bw, grp.tm), lambda i: (layer, i, 0, 0))
    carried = [] if state_kt is None else [state_kt, state_vt]
    n_in = 5
    return pl.pallas_call(
        functools.partial(_qkv_kernel, q_scale=q_scale),
        grid=(grp.n_tiles,),
        in_specs=[
            grp.x_spec(d),
            grp.mod_spec(layer, 0),
            grp.mod_spec(layer, 1),
            _layer_spec((1, d), layer),
            _layer_spec((d, 3 * bw), layer),
        ] + [pl.BlockSpec(memory_space=pl.ANY)] * len(carried),
        out_specs=[o_spec, o_spec, o_spec, t_spec, t_spec],
        out_shape=[
            jax.ShapeDtypeStruct((rows, out_cols), BF16),
            jax.ShapeDtypeStruct((rows, out_cols), kv_dtype),
            jax.ShapeDtypeStruct((rows, out_cols), kv_dtype),
            jax.ShapeDtypeStruct(state_shape, F32),
            jax.ShapeDtypeStruct(state_shape, F32),
        ],
        input_output_aliases={n_in: 3, n_in + 1: 4} if carried else {},
        compiler_params=_params(1),
        name="qkv_proj",
    )(x, mod, mod, norm_w, w_qkv, *carried)


def _rest_kernel(x_ref, sh_ref, sc_ref, g_ref, w_ref, o_ref, h_scr):
    @pl.when(pl.program_id(1) == 0)
    def _():
        h_scr[...] = _norm_mod(x_ref[...], g_ref[...], sc_ref[...], sh_ref[...]).astype(BF16)

    o_ref[...] = _dot(h_scr[...], w_ref[...])


def _rest_proj(grp, layer, x, mod, norm_w, w_in, col0, n_cols, tn):
    d = grp.d
    rows, cols = x.shape
    assert col0 % tn == 0 and n_cols % tn == 0
    nj, j0 = n_cols // tn, col0 // tn
    return pl.pallas_call(
        _rest_kernel,
        grid=(grp.n_tiles, nj),
        in_specs=[
            grp.x_spec(d),
            grp.mod_spec(layer, 0),
            grp.mod_spec(layer, 1),
            _layer_spec((1, d), layer),
            pl.BlockSpec((None, d, tn), lambda i, j: (layer, 0, j0 + j)),
        ],
        out_specs=grp.x_spec_j(tn, nj),
        out_shape=jax.ShapeDtypeStruct((rows, cols // d * n_cols), F32),
        scratch_shapes=[pltpu.VMEM((grp.tm, d), BF16)],
        compiler_params=_params(2),
        name="rest_proj",
    )(x, mod, mod, norm_w, w_in)


def _suffix_prod_sublanes(x):
    sub = lax.broadcasted_iota(jnp.int32, x.shape, 0)
    y = x
    for k in (1, 2, 4):
        shifted = pltpu.roll(y, SUBLANES - k, axis=0)
        y = y * jnp.where(sub + k < SUBLANES, shifted, 1.0)
    return y


def _shift_up_sublanes(y):
    sub = lax.broadcasted_iota(jnp.int32, y.shape, 0)
    return jnp.where(sub + 1 < SUBLANES, pltpu.roll(y, SUBLANES - 1, axis=0), 1.0)


def _sb_tile_permuted(s_tile, carry, diag_offset):
    rest = 0.5 - 0.5 * jnp.tanh(s_tile)
    if diag_offset is not None:
        r = lax.broadcasted_iota(jnp.int32, s_tile.shape, 0)
        q_idx = lax.broadcasted_iota(jnp.int32, s_tile.shape, 1)
        k_idx = (r & 7) * 16 + (r >> 3) + diag_offset
        rest = jnp.where(k_idx < q_idx, rest, 1.0)
    nv = KEY_BLOCK // SUBLANES
    taken = [None] * nv
    acc = None
    for v in range(nv - 1, -1, -1):
        nxt = rest[v * SUBLANES:(v + 1) * SUBLANES]
        nxt = nxt if acc is None else acc * nxt
        taken[v] = (1.0 - nxt) if acc is None else (acc - nxt)
        acc = nxt
    incl = _suffix_prod_sublanes(acc)
    base = _shift_up_sublanes(incl) * carry
    w = jnp.concatenate([t * base for t in taken], axis=0).astype(BF16)
    return w, carry * incl[0:1]


def _attn_prompt_kernel(bias_ref, q_ref, k_ref, v_ref, o_ref, kp_scr, vt_scr, qt_scr, s_a, s_b, w_a, w_b,
                        acc_scr, *, hd):
    hp = pl.program_id(1)
    i = pl.program_id(2)
    tq = q_ref.shape[0]
    n_chunks = k_ref.shape[0] // tq
    halves = tq // KEY_BLOCK

    @pl.when(i == 0)
    def _():
        r = lax.broadcasted_iota(jnp.int32, (KEY_BLOCK, KEY_BLOCK), 0)
        j = lax.broadcasted_iota(jnp.int32, (KEY_BLOCK, KEY_BLOCK), 1)
        perm = jnp.where(j == (r & 7) * 16 + (r >> 3), 1.0, 0.0).astype(BF16)
        ones0 = jnp.where((j >= hd) & (j < hd + 2), 1.0, 0.0)
        ones1 = jnp.where(j < 2, 1.0, 0.0)

        def body(c, carry):
            for hf in range(halves):
                rows = pl.ds(pl.multiple_of(c * tq + hf * KEY_BLOCK, KEY_BLOCK), KEY_BLOCK)
                kp = _dot(perm, k_ref[rows, :])
                vp = _dot(perm, v_ref[rows, :])
                dst = slice(hf * KEY_BLOCK, (hf + 1) * KEY_BLOCK)
                kp_scr[0, c, dst, :] = jnp.where(j < hd, kp, ones0).astype(BF16)
                kp_scr[1, c, dst, :] = jnp.where(j >= hd, kp, ones1).astype(BF16)
                vt_scr[c, :, dst] = vp.T.astype(BF16)
            return carry

        lax.fori_loop(0, n_chunks, body, 0)

    q = q_ref[...].astype(F32)
    lane = lax.broadcasted_iota(jnp.int32, q.shape, 1)
    for h in range(2):
        b = jnp.full(q.shape, 0.5 * bias_ref[2 * hp + h], F32)
        b_hi = b.astype(BF16).astype(F32)
        b_lo = b - b_hi
        own = (lane < hd) if h == 0 else (lane >= hd)
        first = hd if h == 0 else 0
        qm = jnp.where(own, q, jnp.where(lane == first, b_hi, jnp.where(lane == first + 1, b_lo, 0.0)))
        qt_scr[h] = qm.T.astype(BF16)
    acc_scr[...] = jnp.zeros_like(acc_scr)

    def scores(k, s_buf):
        c = jnp.maximum(i - k, 0)
        for h in range(2):
            s_buf[h] = _dot(kp_scr[h, c], qt_scr[h])

    def scan(s_buf, w_buf, diag, carry):
        out = []
        for h in range(2):
            cr = carry[h]
            for hf in range(halves - 1, -1, -1):
                rows = slice(hf * KEY_BLOCK, (hf + 1) * KEY_BLOCK)
                w, cr = _sb_tile_permuted(s_buf[h, rows, :], cr,
                                          hf * KEY_BLOCK if diag else None)
                w_buf[h, rows, :] = w
            out.append(cr)
        return tuple(out)

    def values(k, w_buf):
        vt_blk = vt_scr[i - k]
        for h in range(2):
            acc_scr[h] += _dot(vt_blk, w_buf[h])

    n = i + 1
    ones = jnp.ones((1, tq), F32)
    scores(0, s_a)
    scores(1, s_b)
    carry = scan(s_a, w_a, True, (ones, ones))

    def pair(m, cr):
        k = 2 * m
        scores(k, s_a)
        cr = scan(s_b, w_b, False, cr)
        values(k - 2, w_a)
        scores(k + 1, s_b)
        cr = scan(s_a, w_a, False, cr)
        values(k - 1, w_b)
        return cr

    carry = lax.fori_loop(1, (n - 1) // 2 + 1, pair, carry)

    @pl.when(n % 2 == 0)
    def _():
        scan(s_b, w_b, False, carry)
        values(n - 2, w_a)
        values(n - 1, w_b)

    @pl.when(n % 2 == 1)
    def _():
        values(n - 1, w_a)

    row = lax.broadcasted_iota(jnp.int32, (2 * hd, tq), 0)
    res = jnp.where(row < hd, acc_scr[0], acc_scr[1])
    o_ref[...] = res.T.astype(o_ref.dtype)


def _attn_prompt(q, k, v, bias, hd):
    b, t, c = q.shape
    assert 2 * hd == LANES and t % Q_TILE == 0
    n_chunks = t // Q_TILE
    qo_spec = pl.BlockSpec((None, Q_TILE, LANES), lambda bi, hp, i: (bi, i, hp))
    kv_spec = pl.BlockSpec((None, t, LANES), lambda bi, hp, i: (bi, 0, hp))
    return pl.pallas_call(
        functools.partial(_attn_prompt_kernel, hd=hd),
        grid=(b, c // LANES, n_chunks),
        in_specs=[pl.BlockSpec(memory_space=pltpu.SMEM), qo_spec, kv_spec, kv_spec],
        out_specs=qo_spec,
        out_shape=jax.ShapeDtypeStruct((b, t, c), BF16),
        scratch_shapes=[
            pltpu.VMEM((2, n_chunks, Q_TILE, LANES), BF16),
            pltpu.VMEM((n_chunks, LANES, Q_TILE), BF16),
            pltpu.VMEM((2, LANES, Q_TILE), BF16),
            pltpu.VMEM((2, Q_TILE, Q_TILE), F32),
            pltpu.VMEM((2, Q_TILE, Q_TILE), F32),
            pltpu.VMEM((2, Q_TILE, Q_TILE), BF16),
            pltpu.VMEM((2, Q_TILE, Q_TILE), BF16),
            pltpu.VMEM((2, LANES, Q_TILE), F32),
        ],
        compiler_params=_params(3),
        name="attn_prompt",
    )(bias, q, k, v)


def _suffix_prod_lanes(x):
    n = x.shape[1]
    lane = lax.broadcasted_iota(jnp.int32, x.shape, 1)
    y = x
    k = 1
    while k < n:
        shifted = pltpu.roll(y, n - k, axis=1)
        y = y * jnp.where(lane + k < n, shifted, 1.0)
        k *= 2
    return y


def _attn_decode_kernel(pt_ref, bias_ref, q_ref, kn_ref, vn_ref, *rest, hd, n_pages):
    del pt_ref
    k_pages = rest[:n_pages]
    v_pages = rest[n_pages:2 * n_pages]
    o_ref = rest[2 * n_pages]
    s_len, c = q_ref.shape
    n_heads = c // hd
    nrow = s_len * n_heads

    head_row = lax.broadcasted_iota(jnp.int32, (n_heads, c), 0)
    lane = lax.broadcasted_iota(jnp.int32, (n_heads, c), 1)
    head_mask = (lane >= head_row * hd) & (lane < (head_row + 1) * hd)

    q = q_ref[...].astype(F32)
    q_rows = jnp.concatenate(
        [jnp.where(head_mask, jnp.broadcast_to(q[t:t + 1], (n_heads, c)), 0.0) for t in range(s_len)],
        axis=0).astype(BF16)
    bias_half = 0.5 * bias_ref[...]

    pad = jnp.zeros((KEY_BLOCK - s_len, c), F32)
    k_new = jnp.concatenate([kn_ref[...], pad], axis=0).astype(BF16)
    v_new = jnp.concatenate([vn_ref[...], pad], axis=0).astype(BF16)
    row = lax.broadcasted_iota(jnp.int32, (nrow, KEY_BLOCK), 0)
    key = lax.broadcasted_iota(jnp.int32, (nrow, KEY_BLOCK), 1)
    own_valid = key * n_heads + n_heads <= row

    pages = range(n_pages - 1, -1, -1)
    def rest_of(s):
        return 0.5 - 0.5 * jnp.tanh(s + bias_half)

    rest_all = jnp.concatenate(
        [jnp.where(own_valid, rest_of(_dot_nt(q_rows, k_new)), 1.0)]
        + [rest_of(_dot(q_rows, k_pages[p][...].reshape(c, KEY_BLOCK).astype(BF16))) for p in pages],
        axis=0)
    incl = _suffix_prod_lanes(rest_all)
    lane = lax.broadcasted_iota(jnp.int32, incl.shape, 1)
    later = jnp.where(lane + 1 < KEY_BLOCK, pltpu.roll(incl, KEY_BLOCK - 1, axis=1), 1.0)
    taken = later - incl
    total = jnp.broadcast_to(incl[:, 0:1], incl.shape)
    carry = None
    acc = None
    for j in range(n_pages + 1):
        rows = slice(j * nrow, (j + 1) * nrow)
        w = (taken[rows] if carry is None else taken[rows] * carry).astype(BF16)
        carry = total[rows] if carry is None else carry * total[rows]
        if j == 0:
            acc = _dot(w, v_new)
        else:
            acc = acc + _dot_nt(w, v_pages[n_pages - j][...].reshape(c, KEY_BLOCK).astype(BF16))

    outs = [jnp.sum(jnp.where(head_mask, acc[t * n_heads:(t + 1) * n_heads], 0.0), axis=0, keepdims=True)
            for t in range(s_len)]
    o_ref[...] = jnp.concatenate(outs, axis=0)


def _attn_decode(layer, q, k_new, v_new, cache_kt, cache_vt, page_table, bias_rows, hd):
    nb, s_len, c = q.shape
    n_pages = page_table.shape[1]
    n_heads, page = cache_kt.shape[2], cache_kt.shape[4]
    nrow = s_len * n_heads
    assert page == KEY_BLOCK and n_heads == SUBLANES and n_heads * hd == c
    seq_spec = pl.BlockSpec((None, s_len, c), lambda b, pt: (b, 0, 0))

    def page_spec(p):
        return pl.BlockSpec((None, None, n_heads, hd, page), lambda b, pt: (layer, pt[b, p], 0, 0, 0))

    grid_spec = pltpu.PrefetchScalarGridSpec(
        num_scalar_prefetch=1,
        grid=(nb,),
        in_specs=[pl.BlockSpec((None, nrow, KEY_BLOCK), lambda b, pt: (layer, 0, 0)),
                  seq_spec, seq_spec, seq_spec]
        + [page_spec(p) for p in range(n_pages)] + [page_spec(p) for p in range(n_pages)],
        out_specs=seq_spec,
    )
    return pl.pallas_call(
        functools.partial(_attn_decode_kernel, hd=hd, n_pages=n_pages),
        grid_spec=grid_spec,
        out_shape=jax.ShapeDtypeStruct((nb, s_len, c), F32),
        compiler_params=_params(1),
        name="attn_decode",
    )(page_table, bias_rows, q, k_new, v_new, *([cache_kt] * n_pages), *([cache_vt] * n_pages))


def _merge(x, ys, gates, wb_ref, wo_ref, g1):
    merged = None
    for i, (y, gate) in enumerate(zip(ys, gates)):
        term = _sigmoid(gate) * _dot(y.astype(BF16), wb_ref[i])
        merged = term if merged is None else merged + term
    return x + g1 * _dot(merged.astype(BF16), wo_ref[...])


def _pool_count(pos, win):
    return jnp.minimum(pos + 1.0, float(win))


def _branch_gates(x, nw_ref, sc_ref, sh_ref, wg_ref, d):
    h = _norm_mod(x, nw_ref[...], sc_ref[...], sh_ref[...]).astype(BF16)
    return [_dot(h, wg_ref[:, i * d:(i + 1) * d]) for i in range(N_BRANCH)]


def _mixer_prompt_kernel(x_ref, ya_ref, cv_ref, cvh_ref, pi_ref, pih_ref, su_ref, sv_ref,
                         sh_ref, sc_ref, nw_ref, wg_ref, cw_ref, pw_ref, ps_ref, sw_ref, sb_ref,
                         sn_ref, wb_ref, wo_ref, m1_ref, o_ref, cst_ref, pst_ref, cbuf, pbuf,
                         *, bw, tiles_per_seq):
    tm = x_ref.shape[0]
    first = (pl.program_id(0) % tiles_per_seq) == 0
    hc = cvh_ref.shape[0]
    hp = pih_ref.shape[0]

    cv = cv_ref[...]
    cx = cv[:, 2 * bw:3 * bw] * cv[:, 0:bw]
    cvh = cvh_ref[...]
    cbuf[0:hc, :] = jnp.where(first, 0.0, cvh[:, 2 * bw:3 * bw] * cvh[:, 0:bw])
    cbuf[hc:hc + tm, :] = cx
    cw = cw_ref[...]
    conv = cw[0:1] * cbuf[pl.ds(hc - 2, tm), :] + cw[1:2] * cbuf[pl.ds(hc - 1, tm), :] + cw[2:3] * cx
    y_b = cv[:, bw:2 * bw] * conv
    cst_ref[...] = cx[tm - hc:tm]

    pin = pi_ref[...]
    pbuf[0:hp, :] = jnp.where(first, 0.0, pih_ref[...])
    pbuf[hp:hp + tm, :] = pin
    pst_ref[...] = pin[tm - hp:tm]
    pos = (lax.broadcasted_iota(jnp.int32, (tm, 1), 0)
           + (pl.program_id(0) % tiles_per_seq) * tm).astype(F32)
    gw = bw // len(POOL_WINDOWS)
    y_c = []
    for g, win in enumerate(POOL_WINDOWS):
        lanes = slice(g * gw, (g + 1) * gw)
        wsum = pin[:, lanes]
        for kk in range(1, win):
            wsum = wsum + pbuf[pl.ds(hp - kk, tm), lanes]
        diff = wsum / _pool_count(pos, win) - pin[:, lanes]
        y_c.append(_dot(diff.astype(BF16), pw_ref[g]))
    y_c = jnp.concatenate(y_c, axis=1) * ps_ref[...]

    vn = _rms(sv_ref[...], sn_ref[...])
    n_groups = sw_ref.shape[0] // SG_CHUNK
    gwd = bw // n_groups
    r = lax.broadcasted_iota(jnp.int32, sw_ref.shape, 0) & (SG_CHUNK - 1)
    s_idx = lax.broadcasted_iota(jnp.int32, sw_ref.shape, 1)
    w_tril = jnp.where(s_idx <= r, sw_ref[...], 0.0).astype(BF16)
    lane_grp = lax.broadcasted_iota(jnp.int32, (SG_CHUNK, bw), 1)
    mixed = []
    for n in range(tm // SG_CHUNK):
        full = _dot(w_tril, vn[n * SG_CHUNK:(n + 1) * SG_CHUNK].astype(BF16))
        m = None
        for g in range(n_groups):
            blk = full[g * SG_CHUNK:(g + 1) * SG_CHUNK]
            sel = jnp.where((lane_grp >= g * gwd) & (lane_grp < (g + 1) * gwd), blk, 0.0)
            m = sel if m is None else m + sel
        mixed.append(m + sb_ref[...])
    y_d = su_ref[...] * jnp.concatenate(mixed, axis=0)

    x = x_ref[...]
    gates = _branch_gates(x, nw_ref, sc_ref, sh_ref, wg_ref, x.shape[1])
    o_ref[...] = _merge(x, (ya_ref[...], y_b, y_c, y_d), gates, wb_ref, wo_ref, m1_ref[...])


def _mixer_prompt(grp, layer, x, ya, rest, mod, norm_w, w_gate, conv_w, pool_w, pool_scale, sg_w, sg_bt,
                  sg_norm, w_branch, w_o, bw, n_seq):
    d, tm, tps = grp.d, grp.tm, grp.tps
    rows = x.shape[0]
    hc, hp = SUBLANES, 2 * SUBLANES
    assert hc >= CONV_TAPS - 1 and hp >= POOL_PAD and tm % SG_CHUNK == 0

    def halo(h, width, col):
        per = tm // h
        return pl.BlockSpec((h, width), lambda i: (jnp.maximum(i * per - 1, 0), col))

    n_groups = sg_w.shape[1]
    return pl.pallas_call(
        functools.partial(_mixer_prompt_kernel, bw=bw, tiles_per_seq=tps),
        grid=(grp.n_tiles,),
        in_specs=[
            grp.x_spec(d), grp.x_spec(bw),
            grp.x_spec(3 * bw, col=0), halo(hc, 3 * bw, 0),
            grp.x_spec(bw, col=3), halo(hp, bw, 3),
            grp.x_spec(bw, col=4), grp.x_spec(bw, col=5),
            grp.mod_spec(layer, 0), grp.mod_spec(layer, 1),
            _layer_spec((1, d), layer),
            _layer_spec(w_gate.shape[1:], layer),
            _layer_spec((CONV_TAPS, bw), layer),
            _layer_spec(pool_w.shape[1:], layer),
            _layer_spec((1, bw), layer),
            _layer_spec((n_groups * SG_CHUNK, SG_CHUNK), layer),
            _layer_spec((SG_CHUNK, bw), layer),
            _layer_spec((1, bw), layer),
            _layer_spec(w_branch.shape[1:], layer),
            _layer_spec(w_o.shape[1:], layer),
            grp.mod_spec(layer, 2),
        ],
        out_specs=[
            grp.x_spec(d),
            pl.BlockSpec((None, hc, bw), lambda i: (i // tps, 0, 0)),
            pl.BlockSpec((None, hp, bw), lambda i: (i // tps, 0, 0)),
        ],
        out_shape=[
            jax.ShapeDtypeStruct((rows, d), F32),
            jax.ShapeDtypeStruct((n_seq, hc, bw), F32),
            jax.ShapeDtypeStruct((n_seq, hp, bw), F32),
        ],
        scratch_shapes=[pltpu.VMEM((hc + tm, bw), F32), pltpu.VMEM((hp + tm, bw), F32)],
        compiler_params=_params(1),
        name="mixer_prompt",
    )(x, ya, rest, rest, rest, rest, rest, rest, mod, mod, norm_w, w_gate,
      conv_w, pool_w, pool_scale, sg_w.reshape(sg_w.shape[0], n_groups * SG_CHUNK, SG_CHUNK),
      sg_bt, sg_norm, w_branch, w_o, mod)


def _mixer_decode_kernel(x_ref, ya_ref, r_ref, cs_ref, ps_ref, sh_ref, sc_ref, nw_ref, wg_ref,
                         cw_ref, pw_ref, psc_ref, swv_ref, sbv_ref, sn_ref, wb_ref, wo_ref, m1_ref,
                         o_ref, cst_ref, pst_ref, sgv_ref, *, bw, d, s_len, pos0):
    n_rest = r_ref.shape[1] // s_len

    def rest(t, off, width):
        return r_ref[:, t * n_rest + off:t * n_rest + off + width]

    cw = cw_ref[...]
    cx = [rest(t, 2 * bw, bw) * rest(t, 0, bw) for t in range(s_len)]
    taps = CONV_TAPS - 1
    xp = [cs_ref[:, i * bw:(i + 1) * bw] for i in range(taps)] + cx
    y_b = []
    for t in range(s_len):
        conv = cw[0:1] * xp[t]
        for i in range(1, CONV_TAPS):
            conv = conv + cw[i:i + 1] * xp[t + i]
        y_b.append(rest(t, bw, bw) * conv)
    for i in range(taps):
        cst_ref[:, i * bw:(i + 1) * bw] = xp[len(xp) - taps + i]

    pin = [rest(t, 3 * bw, bw) for t in range(s_len)]
    pp = [ps_ref[:, i * bw:(i + 1) * bw] for i in range(POOL_PAD)] + pin
    for i in range(POOL_PAD):
        pst_ref[:, i * bw:(i + 1) * bw] = pp[len(pp) - POOL_PAD + i]
    gw = bw // len(POOL_WINDOWS)
    diffs = [[] for _ in POOL_WINDOWS]
    for t in range(s_len):
        for g, win in enumerate(POOL_WINDOWS):
            lanes = slice(g * gw, (g + 1) * gw)
            wsum = pp[POOL_PAD + t][:, lanes]
            for kk in range(1, win):
                wsum = wsum + pp[POOL_PAD + t - kk][:, lanes]
            cnt = float(min(pos0 + t + 1, win))
            diffs[g].append(wsum / cnt - pin[t][:, lanes])
    y_c_groups = [_dot(jnp.concatenate(diffs[g], axis=0).astype(BF16), pw_ref[g])
                  for g in range(len(POOL_WINDOWS))]
    y_c_all = jnp.concatenate(y_c_groups, axis=1) * psc_ref[...]

    tb = x_ref.shape[0]
    vn = [_rms(rest(t, 5 * bw, bw), sn_ref[...]) for t in range(s_len)]
    y_d = []
    for t in range(s_len):
        sgv_ref[:, t * bw:(t + 1) * bw] = vn[t]
        mixed = sbv_ref[t:t + 1]
        for s in range(t + 1):
            mixed = mixed + swv_ref[t * s_len + s:t * s_len + s + 1] * vn[s]
        y_d.append(rest(t, 4 * bw, bw) * mixed)

    cat = lambda parts: jnp.concatenate(parts, axis=0)
    x_all = cat([x_ref[:, t * d:(t + 1) * d] for t in range(s_len)])
    ya_all = cat([ya_ref[:, t * bw:(t + 1) * bw] for t in range(s_len)])
    rep = lambda ref: cat([ref[...]] * s_len)
    h_all = _norm_mod(x_all, nw_ref[...], rep(sc_ref), rep(sh_ref)).astype(BF16)
    gates = [_dot(h_all, wg_ref[:, i * d:(i + 1) * d]) for i in range(N_BRANCH)]
    g1 = rep(m1_ref)
    out = _merge(x_all, (ya_all, cat(y_b), y_c_all, cat(y_d)), gates, wb_ref, wo_ref, g1)
    for t in range(s_len):
        o_ref[:, t * d:(t + 1) * d] = out[t * tb:(t + 1) * tb]


def _mixer_decode(layer, x, ya, rest, state_conv, state_pool, mod, norm_w, w_gate, conv_w, pool_w,
                  pool_scale, sg_wv, sg_bv, sg_norm, w_branch, w_o, bw, d, s_len, pos0, tb):
    nb = x.shape[0]

    def row_spec(width):
        return pl.BlockSpec((tb, width), lambda i: (i, 0))

    def state_spec(width):
        return pl.BlockSpec((None, tb, width), lambda i: (layer, i, 0))

    def mod_spec(chunk):
        return pl.BlockSpec((None, tb, d), lambda i: (layer, i, chunk))

    return pl.pallas_call(
        functools.partial(_mixer_decode_kernel, bw=bw, d=d, s_len=s_len, pos0=pos0),
        grid=(nb // tb,),
        in_specs=[
            row_spec(s_len * d), row_spec(s_len * bw), row_spec(rest.shape[1]),
            state_spec(state_conv.shape[2]), state_spec(state_pool.shape[2]),
            mod_spec(0), mod_spec(1),
            _layer_spec((1, d), layer),
            _layer_spec(w_gate.shape[1:], layer),
            _layer_spec((CONV_TAPS, bw), layer),
            _layer_spec(pool_w.shape[1:], layer),
            _layer_spec((1, bw), layer),
            _layer_spec(sg_wv.shape[1:], layer),
            _layer_spec(sg_bv.shape[1:], layer),
            _layer_spec((1, bw), layer),
            _layer_spec(w_branch.shape[1:], layer),
            _layer_spec(w_o.shape[1:], layer),
            mod_spec(2),
        ],
        out_specs=[row_spec(s_len * d), row_spec(state_conv.shape[2]), row_spec(state_pool.shape[2]),
                   row_spec(s_len * bw)],
        out_shape=[
            jax.ShapeDtypeStruct((nb, s_len * d), F32),
            jax.ShapeDtypeStruct((nb, state_conv.shape[2]), F32),
            jax.ShapeDtypeStruct((nb, state_pool.shape[2]), F32),
            jax.ShapeDtypeStruct((nb, s_len * bw), F32),
        ],
        compiler_params=_params(1),
        name="mixer_decode",
    )(x, ya, rest, state_conv, state_pool, mod, mod, norm_w, w_gate, conv_w, pool_w, pool_scale,
      sg_wv, sg_bv, sg_norm, w_branch, w_o, mod)


def _ffn_kernel(x_ref, sh_ref, sc_ref, gt_ref, g_ref, w1_ref, w2_ref, fw_ref, o_ref, h_scr, acc_scr,
                *, final_norm):
    j = pl.program_id(1)

    @pl.when(j == 0)
    def _():
        h_scr[...] = _norm_mod(x_ref[...], g_ref[...], sc_ref[...], sh_ref[...]).astype(BF16)
        acc_scr[...] = jnp.zeros_like(acc_scr)

    a = jnp.maximum(_dot(h_scr[...], w1_ref[...]), 0.0)
    acc_scr[...] += _dot((a * a).astype(BF16), w2_ref[...])

    @pl.when(j == pl.num_programs(1) - 1)
    def _():
        y = x_ref[...] + gt_ref[...] * acc_scr[...]
        if final_norm:
            y = _rms(y, fw_ref[...])
        o_ref[...] = y


def _ffn(grp, layer, x, mod, norm_w, w1, w2, final_w, final_norm, tf):
    d = grp.d
    dff = w1.shape[-1]
    return pl.pallas_call(
        functools.partial(_ffn_kernel, final_norm=final_norm),
        grid=(grp.n_tiles, dff // tf),
        in_specs=[
            grp.x_spec(d),
            grp.mod_spec(layer, 3), grp.mod_spec(layer, 4), grp.mod_spec(layer, 5),
            _layer_spec((1, d), layer),
            pl.BlockSpec((None, d, tf), lambda i, j: (layer, 0, j)),
            pl.BlockSpec((None, tf, d), lambda i, j: (layer, j, 0)),
            _const_spec((1, d)),
        ],
        out_specs=grp.x_spec(d),
        out_shape=jax.ShapeDtypeStruct(x.shape, F32),
        scratch_shapes=[pltpu.VMEM((grp.tm, d), BF16), pltpu.VMEM((grp.tm, d), F32)],
        compiler_params=_params(2),
        name="ffn",
    )(x, mod, mod, mod, norm_w, w1, w2, final_w)


def _row_tile(t, want):
    tm = min(t, want)
    assert t % tm == 0
    return tm


def kernel(x_prompt, x_sample, cache_k, cache_v, state_conv, state_pool, page_table, c_prompt,
           c_sample, w_ada, b_ada, norm_mix, norm_ffn, w_in, sb_bias, conv_w, pool_w, pool_scale,
           sg_w, sg_b, sg_norm, w_branch, w_o, w_ff1, w_ff2, norm_final):
    n_seq, t_len, d = x_prompt.shape
    nb, s_len, _ = x_sample.shape
    depth = w_ada.shape[0]
    n_heads, hd = cache_k.shape[3], cache_k.shape[4]
    bw = n_heads * hd
    n_pages, page = page_table.shape[1], cache_k.shape[2]
    past_len = n_pages * page
    n_groups = sg_w.shape[1]
    gwd = bw // n_groups
    q_scale = 0.5 * hd ** -0.5

    w_in_b = w_in.astype(BF16)
    w_qkv, w_gate = w_in_b[:, :, :3 * bw], w_in_b[:, :, 9 * bw:]
    w_branch_b, w_o_b = w_branch.astype(BF16), w_o.astype(BF16)
    w_ff1_b, w_ff2_b = w_ff1.astype(BF16), w_ff2.astype(BF16)
    pool_w_b = pool_w.astype(BF16)
    norm_mix3, norm_ffn3 = norm_mix[:, None, :], norm_ffn[:, None, :]
    pool_scale3, sg_norm3 = pool_scale[:, None, :], sg_norm[:, None, :]
    norm_final2 = norm_final[None, :]
    chunk = min(t_len, SG_CHUNK)
    assert chunk == SG_CHUNK and sg_w.shape[2] == SG_CHUNK
    sg_bt = jnp.repeat(jnp.swapaxes(sg_b, 1, 2), gwd, axis=2)
    ls = min(s_len, SG_CHUNK)
    assert ls == s_len
    sg_wv = jnp.repeat(jnp.transpose(sg_w[:, :, :ls, :ls], (0, 2, 3, 1)), gwd, axis=3)
    sg_wv = sg_wv.reshape(depth, ls * ls, bw)
    sg_bv = jnp.repeat(jnp.swapaxes(sg_b[:, :, :ls], 1, 2), gwd, axis=2)
    bias_rows = jnp.broadcast_to(jnp.tile(sb_bias, (1, s_len))[:, :, None],
                                 (depth, s_len * n_heads, KEY_BLOCK))

    bp = -(-n_seq // SUBLANES) * SUBLANES
    c_all = jnp.concatenate([c_sample, c_prompt, jnp.zeros((bp - n_seq, d), F32)], axis=0)
    mod_s, mod_p = _adaln(c_all, w_ada, b_ada, nb, bp)
    mod_p = mod_p.reshape(depth, bp, 1, N_MOD * d)

    xp = x_prompt.reshape(n_seq * t_len, d)
    xs = x_sample.reshape(nb, s_len * d)
    ck = jnp.transpose(cache_k, (0, 1, 3, 4, 2))
    cv = jnp.transpose(cache_v, (0, 1, 3, 4, 2))
    sconv = state_conv.reshape(depth, nb, -1)
    spool = state_pool.reshape(depth, nb, -1)

    def prompt_group(want):
        tm = _row_tile(t_len, want)
        return _Group("prompt", n_seq * t_len // tm, tm, d, t_len // tm)

    sample_group = _Group("sample", s_len, nb, d)
    rest_col0, rest_cols = 3 * bw, 6 * bw
    tn_rest = 3 * bw
    tf = 512

    outs = {k: [] for k in ("cp", "pp", "cs", "ps", "sg")}
    ktp = vtp = kts = vts = None
    for l in range(depth):
        last = l == depth - 1
        gq = prompt_group(512)
        q, k, v, ktp, vtp = _qkv_proj(gq, l, xp, mod_p, norm_mix3, w_qkv, bw, q_scale, BF16,
                                      (depth, n_seq, bw, t_len), ktp, vtp)
        rest = _rest_proj(prompt_group(1024), l, xp, mod_p, norm_mix3, w_in_b, rest_col0, rest_cols,
                          tn_rest)
        ya = _attn_prompt(q.reshape(n_seq, t_len, bw), k.reshape(n_seq, t_len, bw),
                          v.reshape(n_seq, t_len, bw), sb_bias[l], hd)
        gm = prompt_group(256)
        xp, cst, pst = _mixer_prompt(gm, l, xp, ya.reshape(n_seq * t_len, bw), rest, mod_p, norm_mix3,
                                     w_gate, conv_w, pool_w_b, pool_scale3, sg_w, sg_bt, sg_norm3,
                                     w_branch_b, w_o_b, bw, n_seq)
        xp = _ffn(prompt_group(1024), l, xp, mod_p, norm_ffn3, w_ff1_b, w_ff2_b, norm_final2, last, tf)
        outs["cp"].append(cst[:, cst.shape[1] - (CONV_TAPS - 1):])
        outs["pp"].append(pst[:, pst.shape[1] - POOL_PAD:])

        q, k, v, kts, vts = _qkv_proj(sample_group, l, xs, mod_s, norm_mix3, w_qkv, bw, q_scale, F32,
                                      (depth, s_len, bw, nb), kts, vts)
        rest = _rest_proj(sample_group, l, xs, mod_s, norm_mix3, w_in_b, rest_col0, rest_cols, tn_rest)
        ya = _attn_decode(l, q.reshape(nb, s_len, bw), k.reshape(nb, s_len, bw),
                          v.reshape(nb, s_len, bw), ck, cv, page_table, bias_rows, hd)
        xs, cst, pst, sgv = _mixer_decode(l, xs, ya.reshape(nb, s_len * bw), rest, sconv, spool,
                                          mod_s, norm_mix3, w_gate, conv_w, pool_w_b, pool_scale3,
                                          sg_wv, sg_bv, sg_norm3, w_branch_b, w_o_b, bw, d, s_len,
                                          past_len, min(nb, 32))
        xs = _ffn(sample_group, l, xs, mod_s, norm_ffn3, w_ff1_b, w_ff2_b, norm_final2, last, tf)
        outs["cs"].append(cst.reshape(nb, CONV_TAPS - 1, bw))
        outs["ps"].append(pst.reshape(nb, POOL_PAD, bw))
        outs["sg"].append(sgv.reshape(nb, s_len, bw))

    st = {k: jnp.stack(v) for k, v in outs.items()}

    def prompt_state(a):
        return jnp.transpose(a.reshape(depth, n_seq, n_heads, hd, t_len), (0, 1, 4, 2, 3))

    def sample_state(a):
        return jnp.transpose(a.reshape(depth, s_len, n_heads, hd, nb), (0, 4, 1, 2, 3))

    return (xp.reshape(n_seq, t_len, d), xs.reshape(nb, s_len, d), prompt_state(ktp), prompt_state(vtp),
            st["cp"], st["pp"], sample_state(kts), sample_state(vts), st["cs"], st["ps"], st["sg"])
```

```python
import functools

import jax
import jax.numpy as jnp
from jax import lax
from jax.experimental import pallas as pl
from jax.experimental.pallas import tpu as pltpu

F32 = jnp.float32
BF16 = jnp.bfloat16

NORM_EPS = 1e-6
CONV_TAPS = 3
POOL_WINDOWS = (2, 4, 8, 16)
POOL_PAD = max(POOL_WINDOWS) - 1
SG_CHUNK = 128
N_MOD = 6
N_BRANCH = 4

LANES = 128
SUBLANES = 8
KEY_BLOCK = 128
Q_TILE = 256
ATTN_HEAD_PAIRS = 1
VMEM_LIMIT_BYTES = 56 * 1024 * 1024


def _params(n_axes):
    return pltpu.CompilerParams(
        dimension_semantics=("arbitrary",) * n_axes, vmem_limit_bytes=VMEM_LIMIT_BYTES)


def _dot(a, b):
    return jnp.dot(a, b, preferred_element_type=F32)


def _dot_nt(a, b):
    return lax.dot_general(a, b, (((1,), (1,)), ((), ())), preferred_element_type=F32)


def _sigmoid(x):
    return 1.0 / (1.0 + jnp.exp(-x))


def _rms(x, g):
    return x * lax.rsqrt(jnp.mean(x * x, axis=-1, keepdims=True) + NORM_EPS) * g


def _norm_mod(x, g, sc, sh):
    return _rms(x, g) * (1.0 + sc) + sh


def _adaln_kernel(c_ref, w_ref, b_ref, os_ref, op_ref):
    c = c_ref[...]
    a = (c * _sigmoid(c)).astype(BF16)
    r = _dot(a, w_ref[...].astype(BF16)) + b_ref[...]
    ns = os_ref.shape[0]
    os_ref[...] = r[:ns]
    op_ref[...] = r[ns:]


def _adaln(c_all, w_ada, b_ada, n_sample, n_prompt_pad):
    depth, d, n = w_ada.shape
    tn = 1024
    rows = c_all.shape[0]
    return pl.pallas_call(
        _adaln_kernel,
        grid=(depth, n // tn),
        in_specs=[
            pl.BlockSpec((rows, d), lambda l, j: (0, 0)),
            pl.BlockSpec((None, d, tn), lambda l, j: (l, 0, j)),
            pl.BlockSpec((None, 1, tn), lambda l, j: (l, 0, j)),
        ],
        out_specs=[
            pl.BlockSpec((None, n_sample, tn), lambda l, j: (l, 0, j)),
            pl.BlockSpec((None, n_prompt_pad, tn), lambda l, j: (l, 0, j)),
        ],
        out_shape=[
            jax.ShapeDtypeStruct((depth, n_sample, n), F32),
            jax.ShapeDtypeStruct((depth, n_prompt_pad, n), F32),
        ],
        compiler_params=_params(2),
        name="adaln",
    )(c_all, w_ada, b_ada.reshape(depth, 1, n))


class _Group:
    def __init__(self, kind, n_tiles, tm, d, tiles_per_seq=None):
        self.kind, self.n_tiles, self.tm, self.d, self.tps = kind, n_tiles, tm, d, tiles_per_seq

    def x_spec(self, width, col_blocks=1, col=0):
        if self.kind == "prompt":
            return pl.BlockSpec((self.tm, width), lambda i, *_: (i, col))
        return pl.BlockSpec((self.tm, width), lambda i, *_: (0, i * col_blocks + col))

    def x_spec_j(self, width, col_blocks):
        if self.kind == "prompt":
            return pl.BlockSpec((self.tm, width), lambda i, j: (i, j))
        return pl.BlockSpec((self.tm, width), lambda i, j: (0, i * col_blocks + j))

    def mod_spec(self, layer, chunk):
        if self.kind == "prompt":
            tps = self.tps
            return pl.BlockSpec((None, None, 1, self.d), lambda i, *_: (layer, i // tps, 0, chunk))
        return pl.BlockSpec((None, self.tm, self.d), lambda i, *_: (layer, 0, chunk))


def _const_spec(shape):
    nd = len(shape)
    return pl.BlockSpec(shape, lambda *_: (0,) * nd)


def _layer_spec(shape, layer):
    nd = len(shape)
    return pl.BlockSpec((None,) + tuple(shape), lambda *_: (layer,) + (0,) * nd)


def _in_proj_kernel(x_ref, sh_ref, sc_ref, g_ref, w_ref, *rest, q_scale):
    q_ref, k_ref, v_ref, kt_ref, vt_ref, r_ref = rest[-6:]
    h = _norm_mod(x_ref[...], g_ref[...], sc_ref[...], sh_ref[...]).astype(BF16)
    bw = q_ref.shape[1]
    p = _dot(h, w_ref[:, :3 * bw])
    k, v = p[:, bw:2 * bw], p[:, 2 * bw:3 * bw]
    q_ref[...] = (p[:, :bw] * q_scale).astype(q_ref.dtype)
    k_ref[...] = k.astype(k_ref.dtype)
    v_ref[...] = v.astype(v_ref.dtype)
    kt_ref[...] = k.T
    vt_ref[...] = v.T
    r_ref[...] = _dot(h, w_ref[:, 3 * bw:])


def _in_proj(grp, layer, x, mod, norm_w, w_in, bw, n_cols, q_scale, kv_dtype, state_shape, state_kt,
             state_vt):
    d = grp.d
    rows, cols = x.shape
    n_pos = cols // d
    n_rest = n_cols - 3 * bw
    o_spec = grp.x_spec(bw)
    if grp.kind == "prompt":
        tps = grp.tps
        t_spec = pl.BlockSpec((None, None, bw, grp.tm), lambda i: (layer, i // tps, 0, i % tps))
    else:
        t_spec = pl.BlockSpec((None, None, bw, grp.tm), lambda i: (layer, i, 0, 0))
    carried = [] if state_kt is None else [state_kt, state_vt]
    n_in = 5
    return pl.pallas_call(
        functools.partial(_in_proj_kernel, q_scale=q_scale),
        grid=(grp.n_tiles,),
        in_specs=[
            grp.x_spec(d),
            grp.mod_spec(layer, 0),
            grp.mod_spec(layer, 1),
            _layer_spec((1, d), layer),
            _layer_spec((d, n_cols), layer),
        ] + [pl.BlockSpec(memory_space=pl.ANY)] * len(carried),
        out_specs=[o_spec, o_spec, o_spec, t_spec, t_spec, grp.x_spec(n_rest)],
        out_shape=[
            jax.ShapeDtypeStruct((rows, n_pos * bw), BF16),
            jax.ShapeDtypeStruct((rows, n_pos * bw), kv_dtype),
            jax.ShapeDtypeStruct((rows, n_pos * bw), kv_dtype),
            jax.ShapeDtypeStruct(state_shape, F32),
            jax.ShapeDtypeStruct(state_shape, F32),
            jax.ShapeDtypeStruct((rows, n_pos * n_rest), F32),
        ],
        input_output_aliases={n_in: 3, n_in + 1: 4} if carried else {},
        compiler_params=_params(1),
        name="in_proj",
    )(x, mod, mod, norm_w, w_in, *carried)


def _suffix_prod_sublanes(x):
    sub = lax.broadcasted_iota(jnp.int32, x.shape, 0)
    y = x
    for k in (1, 2, 4):
        shifted = pltpu.roll(y, SUBLANES - k, axis=0)
        y = y * jnp.where(sub + k < SUBLANES, shifted, 1.0)
    return y


def _shift_up_sublanes(y):
    sub = lax.broadcasted_iota(jnp.int32, y.shape, 0)
    return jnp.where(sub + 1 < SUBLANES, pltpu.roll(y, SUBLANES - 1, axis=0), 1.0)


def _sb_tile_permuted(s_tile, carry, diag_offset):
    n = s_tile.shape[1]
    if n > LANES:
        parts = [_sb_tile_permuted(s_tile[:, c:c + LANES], carry[:, c:c + LANES],
                                   None if diag_offset is None else diag_offset - c)
                 for c in range(0, n, LANES)]
        return (jnp.concatenate([p[0] for p in parts], axis=1),
                jnp.concatenate([p[1] for p in parts], axis=1))
    if diag_offset is not None and diag_offset >= n:
        return jnp.zeros(s_tile.shape, BF16), carry
    rest = 0.5 - 0.5 * jnp.tanh(s_tile)
    if diag_offset is not None:
        r = lax.broadcasted_iota(jnp.int32, s_tile.shape, 0)
        q_idx = lax.broadcasted_iota(jnp.int32, s_tile.shape, 1)
        k_idx = (r & 7) * 16 + (r >> 3) + diag_offset
        rest = jnp.where(k_idx < q_idx, rest, 1.0)
    nv = KEY_BLOCK // SUBLANES
    taken = [None] * nv
    acc = None
    for v in range(nv - 1, -1, -1):
        nxt = rest[v * SUBLANES:(v + 1) * SUBLANES]
        nxt = nxt if acc is None else acc * nxt
        taken[v] = (1.0 - nxt) if acc is None else (acc - nxt)
        acc = nxt
    incl = _suffix_prod_sublanes(acc)
    base = _shift_up_sublanes(incl) * carry
    w = jnp.concatenate([t * base for t in taken], axis=0).astype(BF16)
    return w, carry * incl[0:1]


def _attn_prompt_kernel(bias_ref, q_ref, k_ref, v_ref, o_ref, kp_scr, vt_scr, qt_scr, s_a, s_b, w_a, w_b,
                        acc_scr, *, hd, n_pairs):
    hp = pl.program_id(1)
    i = pl.program_id(2)
    tq = q_ref.shape[0]
    n_chunks = k_ref.shape[0] // tq
    halves = tq // KEY_BLOCK

    @pl.when(i == 0)
    def _():
        r = lax.broadcasted_iota(jnp.int32, (KEY_BLOCK, KEY_BLOCK), 0)
        j = lax.broadcasted_iota(jnp.int32, (KEY_BLOCK, KEY_BLOCK), 1)
        perm = jnp.where(j == (r & 7) * 16 + (r >> 3), 1.0, 0.0).astype(BF16)
        ones0 = jnp.where((j >= hd) & (j < hd + 2), 1.0, 0.0)
        ones1 = jnp.where(j < 2, 1.0, 0.0)

        def body(c, carry):
            for hf in range(halves):
                rows = pl.ds(pl.multiple_of(c * tq + hf * KEY_BLOCK, KEY_BLOCK), KEY_BLOCK)
                dst = slice(hf * KEY_BLOCK, (hf + 1) * KEY_BLOCK)
                for p in range(n_pairs):
                    cols = slice(p * LANES, (p + 1) * LANES)
                    kp = _dot(perm, k_ref[rows, cols])
                    vp = _dot(perm, v_ref[rows, cols])
                    kp_scr[2 * p, c, dst, :] = jnp.where(j < hd, kp, ones0).astype(BF16)
                    kp_scr[2 * p + 1, c, dst, :] = jnp.where(j >= hd, kp, ones1).astype(BF16)
                    vt_scr[p, c, :, dst] = vp.T.astype(BF16)
            return carry

        lax.fori_loop(0, n_chunks, body, 0)

    lane = lax.broadcasted_iota(jnp.int32, (tq, LANES), 1)
    for p in range(n_pairs):
        q = q_ref[:, p * LANES:(p + 1) * LANES].astype(F32)
        for h in range(2):
            b = jnp.full(q.shape, 0.5 * bias_ref[2 * (hp * n_pairs + p) + h], F32)
            b_hi = b.astype(BF16).astype(F32)
            b_lo = b - b_hi
            own = (lane < hd) if h == 0 else (lane >= hd)
            first = hd if h == 0 else 0
            qm = jnp.where(own, q, jnp.where(lane == first, b_hi, jnp.where(lane == first + 1, b_lo, 0.0)))
            qt_scr[2 * p + h] = qm.T.astype(BF16)
    acc_scr[...] = jnp.zeros_like(acc_scr)

    n_heads = 2 * n_pairs

    def scores(k, s_buf):
        c = jnp.maximum(i - k, 0)
        for h in range(n_heads):
            s_buf[h] = _dot(kp_scr[h, c], qt_scr[h])

    def scan(s_buf, w_buf, diag, carry):
        out = []
        for h in range(n_heads):
            cr = carry[h]
            for hf in range(halves - 1, -1, -1):
                rows = slice(hf * KEY_BLOCK, (hf + 1) * KEY_BLOCK)
                w, cr = _sb_tile_permuted(s_buf[h, rows, :], cr,
                                          hf * KEY_BLOCK if diag else None)
                w_buf[h, rows, :] = w
            out.append(cr)
        return tuple(out)

    def values(k, w_buf):
        for h in range(n_heads):
            acc_scr[h] += _dot(vt_scr[h // 2, i - k], w_buf[h])

    n = i + 1
    ones = jnp.ones((1, tq), F32)
    scores(0, s_a)
    scores(1, s_b)
    carry = scan(s_a, w_a, True, (ones,) * n_heads)

    def pair(m, cr):
        k = 2 * m
        values(k - 2, w_a)
        scores(k, s_a)
        cr = scan(s_b, w_b, False, cr)
        values(k - 1, w_b)
        scores(k + 1, s_b)
        cr = scan(s_a, w_a, False, cr)
        return cr

    carry = lax.fori_loop(1, (n - 1) // 2 + 1, pair, carry)

    @pl.when(n % 2 == 0)
    def _():
        scan(s_b, w_b, False, carry)
        values(n - 2, w_a)
        values(n - 1, w_b)

    @pl.when(n % 2 == 1)
    def _():
        values(n - 1, w_a)

    row = lax.broadcasted_iota(jnp.int32, (2 * hd, tq), 0)
    for p in range(n_pairs):
        res = jnp.where(row < hd, acc_scr[2 * p], acc_scr[2 * p + 1])
        o_ref[:, p * LANES:(p + 1) * LANES] = res.T.astype(o_ref.dtype)


def _attn_prompt(q, k, v, bias, hd):
    b, t, c = q.shape
    n_pairs = ATTN_HEAD_PAIRS
    width = n_pairs * LANES
    assert 2 * hd == LANES and t % Q_TILE == 0 and c % width == 0
    n_chunks = t // Q_TILE
    nh = 2 * n_pairs
    qo_spec = pl.BlockSpec((None, Q_TILE, width), lambda bi, hp, i: (bi, i, hp))
    kv_spec = pl.BlockSpec((None, t, width), lambda bi, hp, i: (bi, 0, hp))
    return pl.pallas_call(
        functools.partial(_attn_prompt_kernel, hd=hd, n_pairs=n_pairs),
        grid=(b, c // width, n_chunks),
        in_specs=[pl.BlockSpec(memory_space=pltpu.SMEM), qo_spec, kv_spec, kv_spec],
        out_specs=qo_spec,
        out_shape=jax.ShapeDtypeStruct((b, t, c), BF16),
        scratch_shapes=[
            pltpu.VMEM((nh, n_chunks, Q_TILE, LANES), BF16),
            pltpu.VMEM((n_pairs, n_chunks, LANES, Q_TILE), BF16),
            pltpu.VMEM((nh, LANES, Q_TILE), BF16),
            pltpu.VMEM((nh, Q_TILE, Q_TILE), F32),
            pltpu.VMEM((nh, Q_TILE, Q_TILE), F32),
            pltpu.VMEM((nh, Q_TILE, Q_TILE), BF16),
            pltpu.VMEM((nh, Q_TILE, Q_TILE), BF16),
            pltpu.VMEM((nh, LANES, Q_TILE), F32),
        ],
        compiler_params=_params(3),
        name="attn_prompt",
    )(bias, q, k, v)


def _suffix_prod_lanes(x):
    n = x.shape[1]
    lane = lax.broadcasted_iota(jnp.int32, x.shape, 1)
    y = x
    k = 1
    while k < n:
        shifted = pltpu.roll(y, n - k, axis=1)
        y = y * jnp.where(lane + k < n, shifted, 1.0)
        k *= 2
    return y


def _attn_decode_kernel(pt_ref, bias_ref, q_ref, kn_ref, vn_ref, *rest, hd, n_pages):
    del pt_ref
    k_pages = rest[:n_pages]
    v_pages = rest[n_pages:2 * n_pages]
    o_ref = rest[2 * n_pages]
    s_len, c = q_ref.shape
    n_heads = c // hd
    nrow = s_len * n_heads

    head_row = lax.broadcasted_iota(jnp.int32, (n_heads, c), 0)
    lane = lax.broadcasted_iota(jnp.int32, (n_heads, c), 1)
    head_mask = (lane >= head_row * hd) & (lane < (head_row + 1) * hd)

    q = q_ref[...].astype(F32)
    q_rows = jnp.concatenate(
        [jnp.where(head_mask, jnp.broadcast_to(q[t:t + 1], (n_heads, c)), 0.0) for t in range(s_len)],
        axis=0).astype(BF16)
    bias_half = 0.5 * bias_ref[...]

    pad = jnp.zeros((KEY_BLOCK - s_len, c), F32)
    k_new = jnp.concatenate([kn_ref[...], pad], axis=0).astype(BF16)
    v_new = jnp.concatenate([vn_ref[...], pad], axis=0).astype(BF16)
    row = lax.broadcasted_iota(jnp.int32, (nrow, KEY_BLOCK), 0)
    key = lax.broadcasted_iota(jnp.int32, (nrow, KEY_BLOCK), 1)
    own_valid = key * n_heads + n_heads <= row

    pages = range(n_pages - 1, -1, -1)
    def rest_of(s):
        return 0.5 - 0.5 * jnp.tanh(s + bias_half)

    rest_all = jnp.concatenate(
        [jnp.where(own_valid, rest_of(_dot_nt(q_rows, k_new)), 1.0)]
        + [rest_of(_dot(q_rows, k_pages[p][...].reshape(c, KEY_BLOCK).astype(BF16))) for p in pages],
        axis=0)
    incl = _suffix_prod_lanes(rest_all)
    lane = lax.broadcasted_iota(jnp.int32, incl.shape, 1)
    later = jnp.where(lane + 1 < KEY_BLOCK, pltpu.roll(incl, KEY_BLOCK - 1, axis=1), 1.0)
    taken = later - incl
    total = jnp.broadcast_to(incl[:, 0:1], incl.shape)
    carry = None
    acc = None
    for j in range(n_pages + 1):
        rows = slice(j * nrow, (j + 1) * nrow)
        w = (taken[rows] if carry is None else taken[rows] * carry).astype(BF16)
        carry = total[rows] if carry is None else carry * total[rows]
        if j == 0:
            acc = _dot(w, v_new)
        else:
            acc = acc + _dot_nt(w, v_pages[n_pages - j][...].reshape(c, KEY_BLOCK).astype(BF16))

    outs = [jnp.sum(jnp.where(head_mask, acc[t * n_heads:(t + 1) * n_heads], 0.0), axis=0, keepdims=True)
            for t in range(s_len)]
    o_ref[...] = jnp.concatenate(outs, axis=0)


def _attn_decode(layer, q, k_new, v_new, cache_kt, cache_vt, page_table, bias_rows, hd):
    nb, s_len, c = q.shape
    n_pages = page_table.shape[1]
    n_heads, page = cache_kt.shape[2], cache_kt.shape[4]
    nrow = s_len * n_heads
    assert page == KEY_BLOCK and n_heads == SUBLANES and n_heads * hd == c
    seq_spec = pl.BlockSpec((None, s_len, c), lambda b, pt: (b, 0, 0))

    def page_spec(p):
        return pl.BlockSpec((None, None, n_heads, hd, page), lambda b, pt: (layer, pt[b, p], 0, 0, 0))

    grid_spec = pltpu.PrefetchScalarGridSpec(
        num_scalar_prefetch=1,
        grid=(nb,),
        in_specs=[pl.BlockSpec((None, nrow, KEY_BLOCK), lambda b, pt: (layer, 0, 0)),
                  seq_spec, seq_spec, seq_spec]
        + [page_spec(p) for p in range(n_pages)] + [page_spec(p) for p in range(n_pages)],
        out_specs=seq_spec,
    )
    return pl.pallas_call(
        functools.partial(_attn_decode_kernel, hd=hd, n_pages=n_pages),
        grid_spec=grid_spec,
        out_shape=jax.ShapeDtypeStruct((nb, s_len, c), F32),
        compiler_params=_params(1),
        name="attn_decode",
    )(page_table, bias_rows, q, k_new, v_new, *([cache_kt] * n_pages), *([cache_vt] * n_pages))


def _merge(x, ys, gates, wb_ref, wo_ref, g1):
    merged = None
    for i, (y, gate) in enumerate(zip(ys, gates)):
        term = _sigmoid(gate) * _dot(y.astype(BF16), wb_ref[i])
        merged = term if merged is None else merged + term
    return x + g1 * _dot(merged.astype(BF16), wo_ref[...])


def _pool_count(pos, win):
    return jnp.minimum(pos + 1.0, float(win))


def _branch_gates(x, nw_ref, sc_ref, sh_ref, wg_ref, d):
    h = _norm_mod(x, nw_ref[...], sc_ref[...], sh_ref[...]).astype(BF16)
    return [_dot(h, wg_ref[:, i * d:(i + 1) * d]) for i in range(N_BRANCH)]


def _mixer_prompt_kernel(x_ref, ya_ref, cv_ref, cvh_ref, pi_ref, pih_ref, su_ref, sv_ref,
                         sh_ref, sc_ref, nw_ref, wg_ref, cw_ref, pw_ref, ps_ref, sw_ref, sb_ref,
                         sn_ref, wb_ref, wo_ref, m1_ref, o_ref, cst_ref, pst_ref, cbuf, pbuf,
                         *, bw, tiles_per_seq):
    tm = x_ref.shape[0]
    first = (pl.program_id(0) % tiles_per_seq) == 0
    hc = cvh_ref.shape[0]
    hp = pih_ref.shape[0]

    cv = cv_ref[...]
    cx = cv[:, 2 * bw:3 * bw] * cv[:, 0:bw]
    cvh = cvh_ref[...]
    cbuf[0:hc, :] = jnp.where(first, 0.0, cvh[:, 2 * bw:3 * bw] * cvh[:, 0:bw])
    cbuf[hc:hc + tm, :] = cx
    cw = cw_ref[...]
    conv = cw[0:1] * cbuf[pl.ds(hc - 2, tm), :] + cw[1:2] * cbuf[pl.ds(hc - 1, tm), :] + cw[2:3] * cx
    y_b = cv[:, bw:2 * bw] * conv
    cst_ref[...] = cx[tm - hc:tm]

    pin = pi_ref[...]
    pbuf[0:hp, :] = jnp.where(first, 0.0, pih_ref[...])
    pbuf[hp:hp + tm, :] = pin
    pst_ref[...] = pin[tm - hp:tm]
    pos = (lax.broadcasted_iota(jnp.int32, (tm, 1), 0)
           + (pl.program_id(0) % tiles_per_seq) * tm).astype(F32)
    gw = bw // len(POOL_WINDOWS)
    y_c = []
    for g, win in enumerate(POOL_WINDOWS):
        lanes = slice(g * gw, (g + 1) * gw)
        wsum = pin[:, lanes]
        for kk in range(1, win):
            wsum = wsum + pbuf[pl.ds(hp - kk, tm), lanes]
        diff = wsum / _pool_count(pos, win) - pin[:, lanes]
        y_c.append(_dot(diff.astype(BF16), pw_ref[g]))
    y_c = jnp.concatenate(y_c, axis=1) * ps_ref[...]

    vn = _rms(sv_ref[...], sn_ref[...])
    n_groups = sw_ref.shape[0] // SG_CHUNK
    gwd = bw // n_groups
    r = lax.broadcasted_iota(jnp.int32, sw_ref.shape, 0) & (SG_CHUNK - 1)
    s_idx = lax.broadcasted_iota(jnp.int32, sw_ref.shape, 1)
    w_tril = jnp.where(s_idx <= r, sw_ref[...], 0.0).astype(BF16)
    lane_grp = lax.broadcasted_iota(jnp.int32, (SG_CHUNK, bw), 1)
    mixed = []
    for n in range(tm // SG_CHUNK):
        full = _dot(w_tril, vn[n * SG_CHUNK:(n + 1) * SG_CHUNK].astype(BF16))
        m = None
        for g in range(n_groups):
            blk = full[g * SG_CHUNK:(g + 1) * SG_CHUNK]
            sel = jnp.where((lane_grp >= g * gwd) & (lane_grp < (g + 1) * gwd), blk, 0.0)
            m = sel if m is None else m + sel
        mixed.append(m + sb_ref[...])
    y_d = su_ref[...] * jnp.concatenate(mixed, axis=0)

    x = x_ref[...]
    gates = _branch_gates(x, nw_ref, sc_ref, sh_ref, wg_ref, x.shape[1])
    o_ref[...] = _merge(x, (ya_ref[...], y_b, y_c, y_d), gates, wb_ref, wo_ref, m1_ref[...])


def _mixer_prompt(grp, layer, x, ya, rest, mod, norm_w, w_gate, conv_w, pool_w, pool_scale, sg_w, sg_bt,
                  sg_norm, w_branch, w_o, bw, n_seq):
    d, tm, tps = grp.d, grp.tm, grp.tps
    rows = x.shape[0]
    hc, hp = SUBLANES, 2 * SUBLANES
    assert hc >= CONV_TAPS - 1 and hp >= POOL_PAD and tm % SG_CHUNK == 0

    def halo(h, width, col):
        per = tm // h
        return pl.BlockSpec((h, width), lambda i: (jnp.maximum(i * per - 1, 0), col))

    n_groups = sg_w.shape[1]
    return pl.pallas_call(
        functools.partial(_mixer_prompt_kernel, bw=bw, tiles_per_seq=tps),
        grid=(grp.n_tiles,),
        in_specs=[
            grp.x_spec(d), grp.x_spec(bw),
            grp.x_spec(3 * bw, col=0), halo(hc, 3 * bw, 0),
            grp.x_spec(bw, col=3), halo(hp, bw, 3),
            grp.x_spec(bw, col=4), grp.x_spec(bw, col=5),
            grp.mod_spec(layer, 0), grp.mod_spec(layer, 1),
            _layer_spec((1, d), layer),
            _layer_spec(w_gate.shape[1:], layer),
            _layer_spec((CONV_TAPS, bw), layer),
            _layer_spec(pool_w.shape[1:], layer),
            _layer_spec((1, bw), layer),
            _layer_spec((n_groups * SG_CHUNK, SG_CHUNK), layer),
            _layer_spec((SG_CHUNK, bw), layer),
            _layer_spec((1, bw), layer),
            _layer_spec(w_branch.shape[1:], layer),
            _layer_spec(w_o.shape[1:], layer),
            grp.mod_spec(layer, 2),
        ],
        out_specs=[
            grp.x_spec(d),
            pl.BlockSpec((None, hc, bw), lambda i: (i // tps, 0, 0)),
            pl.BlockSpec((None, hp, bw), lambda i: (i // tps, 0, 0)),
        ],
        out_shape=[
            jax.ShapeDtypeStruct((rows, d), F32),
            jax.ShapeDtypeStruct((n_seq, hc, bw), F32),
            jax.ShapeDtypeStruct((n_seq, hp, bw), F32),
        ],
        scratch_shapes=[pltpu.VMEM((hc + tm, bw), F32), pltpu.VMEM((hp + tm, bw), F32)],
        compiler_params=_params(1),
        name="mixer_prompt",
    )(x, ya, rest, rest, rest, rest, rest, rest, mod, mod, norm_w, w_gate,
      conv_w, pool_w, pool_scale, sg_w.reshape(sg_w.shape[0], n_groups * SG_CHUNK, SG_CHUNK),
      sg_bt, sg_norm, w_branch, w_o, mod)


def _mixer_decode_kernel(x_ref, ya_ref, r_ref, cs_ref, ps_ref, sh_ref, sc_ref, nw_ref, wg_ref,
                         cw_ref, pw_ref, psc_ref, swv_ref, sbv_ref, sn_ref, wb_ref, wo_ref, m1_ref,
                         o_ref, cst_ref, pst_ref, sgv_ref, *, bw, d, s_len, pos0):
    n_rest = r_ref.shape[1] // s_len

    def rest(t, off, width):
        return r_ref[:, t * n_rest + off:t * n_rest + off + width]

    cw = cw_ref[...]
    cx = [rest(t, 2 * bw, bw) * rest(t, 0, bw) for t in range(s_len)]
    taps = CONV_TAPS - 1
    xp = [cs_ref[:, i * bw:(i + 1) * bw] for i in range(taps)] + cx
    y_b = []
    for t in range(s_len):
        conv = cw[0:1] * xp[t]
        for i in range(1, CONV_TAPS):
            conv = conv + cw[i:i + 1] * xp[t + i]
        y_b.append(rest(t, bw, bw) * conv)
    for i in range(taps):
        cst_ref[:, i * bw:(i + 1) * bw] = xp[len(xp) - taps + i]

    pin = [rest(t, 3 * bw, bw) for t in range(s_len)]
    pp = [ps_ref[:, i * bw:(i + 1) * bw] for i in range(POOL_PAD)] + pin
    for i in range(POOL_PAD):
        pst_ref[:, i * bw:(i + 1) * bw] = pp[len(pp) - POOL_PAD + i]
    gw = bw // len(POOL_WINDOWS)
    diffs = [[] for _ in POOL_WINDOWS]
    for t in range(s_len):
        for g, win in enumerate(POOL_WINDOWS):
            lanes = slice(g * gw, (g + 1) * gw)
            wsum = pp[POOL_PAD + t][:, lanes]
            for kk in range(1, win):
                wsum = wsum + pp[POOL_PAD + t - kk][:, lanes]
            cnt = float(min(pos0 + t + 1, win))
            diffs[g].append(wsum / cnt - pin[t][:, lanes])
    y_c_groups = [_dot(jnp.concatenate(diffs[g], axis=0).astype(BF16), pw_ref[g])
                  for g in range(len(POOL_WINDOWS))]
    y_c_all = jnp.concatenate(y_c_groups, axis=1) * psc_ref[...]

    tb = x_ref.shape[0]
    vn = [_rms(rest(t, 5 * bw, bw), sn_ref[...]) for t in range(s_len)]
    y_d = []
    for t in range(s_len):
        sgv_ref[:, t * bw:(t + 1) * bw] = vn[t]
        mixed = sbv_ref[t:t + 1]
        for s in range(t + 1):
            mixed = mixed + swv_ref[t * s_len + s:t * s_len + s + 1] * vn[s]
        y_d.append(rest(t, 4 * bw, bw) * mixed)

    cat = lambda parts: jnp.concatenate(parts, axis=0)
    x_all = cat([x_ref[:, t * d:(t + 1) * d] for t in range(s_len)])
    ya_all = cat([ya_ref[:, t * bw:(t + 1) * bw] for t in range(s_len)])
    rep = lambda ref: cat([ref[...]] * s_len)
    h_all = _norm_mod(x_all, nw_ref[...], rep(sc_ref), rep(sh_ref)).astype(BF16)
    gates = [_dot(h_all, wg_ref[:, i * d:(i + 1) * d]) for i in range(N_BRANCH)]
    g1 = rep(m1_ref)
    out = _merge(x_all, (ya_all, cat(y_b), y_c_all, cat(y_d)), gates, wb_ref, wo_ref, g1)
    for t in range(s_len):
        o_ref[:, t * d:(t + 1) * d] = out[t * tb:(t + 1) * tb]


def _mixer_decode(layer, x, ya, rest, state_conv, state_pool, mod, norm_w, w_gate, conv_w, pool_w,
                  pool_scale, sg_wv, sg_bv, sg_norm, w_branch, w_o, bw, d, s_len, pos0, tb):
    nb = x.shape[0]

    def row_spec(width):
        return pl.BlockSpec((tb, width), lambda i: (i, 0))

    def state_spec(width):
        return pl.BlockSpec((None, tb, width), lambda i: (layer, i, 0))

    def mod_spec(chunk):
        return pl.BlockSpec((None, tb, d), lambda i: (layer, i, chunk))

    return pl.pallas_call(
        functools.partial(_mixer_decode_kernel, bw=bw, d=d, s_len=s_len, pos0=pos0),
        grid=(nb // tb,),
        in_specs=[
            row_spec(s_len * d), row_spec(s_len * bw), row_spec(rest.shape[1]),
            state_spec(state_conv.shape[2]), state_spec(state_pool.shape[2]),
            mod_spec(0), mod_spec(1),
            _layer_spec((1, d), layer),
            _layer_spec(w_gate.shape[1:], layer),
            _layer_spec((CONV_TAPS, bw), layer),
            _layer_spec(pool_w.shape[1:], layer),
            _layer_spec((1, bw), layer),
            _layer_spec(sg_wv.shape[1:], layer),
            _layer_spec(sg_bv.shape[1:], layer),
            _layer_spec((1, bw), layer),
            _layer_spec(w_branch.shape[1:], layer),
            _layer_spec(w_o.shape[1:], layer),
            mod_spec(2),
        ],
        out_specs=[row_spec(s_len * d), row_spec(state_conv.shape[2]), row_spec(state_pool.shape[2]),
                   row_spec(s_len * bw)],
        out_shape=[
            jax.ShapeDtypeStruct((nb, s_len * d), F32),
            jax.ShapeDtypeStruct((nb, state_conv.shape[2]), F32),
            jax.ShapeDtypeStruct((nb, state_pool.shape[2]), F32),
            jax.ShapeDtypeStruct((nb, s_len * bw), F32),
        ],
        compiler_params=_params(1),
        name="mixer_decode",
    )(x, ya, rest, state_conv, state_pool, mod, mod, norm_w, w_gate, conv_w, pool_w, pool_scale,
      sg_wv, sg_bv, sg_norm, w_branch, w_o, mod)


def _ffn_kernel(x_ref, sh_ref, sc_ref, gt_ref, g_ref, w1_ref, w2_ref, fw_ref, o_ref, h_scr, acc_scr,
                *, final_norm):
    j = pl.program_id(1)

    @pl.when(j == 0)
    def _():
        h_scr[...] = _norm_mod(x_ref[...], g_ref[...], sc_ref[...], sh_ref[...]).astype(BF16)
        acc_scr[...] = jnp.zeros_like(acc_scr)

    a = jnp.maximum(_dot(h_scr[...], w1_ref[...]), 0.0)
    acc_scr[...] += _dot((a * a).astype(BF16), w2_ref[...])

    @pl.when(j == pl.num_programs(1) - 1)
    def _():
        y = x_ref[...] + gt_ref[...] * acc_scr[...]
        if final_norm:
            y = _rms(y, fw_ref[...])
        o_ref[...] = y


def _ffn(grp, layer, x, mod, norm_w, w1, w2, final_w, final_norm, tf):
    d = grp.d
    dff = w1.shape[-1]
    return pl.pallas_call(
        functools.partial(_ffn_kernel, final_norm=final_norm),
        grid=(grp.n_tiles, dff // tf),
        in_specs=[
            grp.x_spec(d),
            grp.mod_spec(layer, 3), grp.mod_spec(layer, 4), grp.mod_spec(layer, 5),
            _layer_spec((1, d), layer),
            pl.BlockSpec((None, d, tf), lambda i, j: (layer, 0, j)),
            pl.BlockSpec((None, tf, d), lambda i, j: (layer, j, 0)),
            _const_spec((1, d)),
        ],
        out_specs=grp.x_spec(d),
        out_shape=jax.ShapeDtypeStruct(x.shape, F32),
        scratch_shapes=[pltpu.VMEM((grp.tm, d), BF16), pltpu.VMEM((grp.tm, d), F32)],
        compiler_params=_params(2),
        name="ffn",
    )(x, mod, mod, mod, norm_w, w1, w2, final_w)


def _row_tile(t, want):
    tm = min(t, want)
    assert t % tm == 0
    return tm


def kernel(x_prompt, x_sample, cache_k, cache_v, state_conv, state_pool, page_table, c_prompt,
           c_sample, w_ada, b_ada, norm_mix, norm_ffn, w_in, sb_bias, conv_w, pool_w, pool_scale,
           sg_w, sg_b, sg_norm, w_branch, w_o, w_ff1, w_ff2, norm_final):
    n_seq, t_len, d = x_prompt.shape
    nb, s_len, _ = x_sample.shape
    depth = w_ada.shape[0]
    n_heads, hd = cache_k.shape[3], cache_k.shape[4]
    bw = n_heads * hd
    n_pages, page = page_table.shape[1], cache_k.shape[2]
    past_len = n_pages * page
    n_groups = sg_w.shape[1]
    gwd = bw // n_groups
    q_scale = 0.5 * hd ** -0.5

    w_in_b = w_in.astype(BF16)
    n_proj = 9 * bw
    w_gate = w_in_b[:, :, n_proj:]
    w_branch_b, w_o_b = w_branch.astype(BF16), w_o.astype(BF16)
    w_ff1_b, w_ff2_b = w_ff1.astype(BF16), w_ff2.astype(BF16)
    pool_w_b = pool_w.astype(BF16)
    norm_mix3, norm_ffn3 = norm_mix[:, None, :], norm_ffn[:, None, :]
    pool_scale3, sg_norm3 = pool_scale[:, None, :], sg_norm[:, None, :]
    norm_final2 = norm_final[None, :]
    chunk = min(t_len, SG_CHUNK)
    assert chunk == SG_CHUNK and sg_w.shape[2] == SG_CHUNK
    sg_bt = jnp.repeat(jnp.swapaxes(sg_b, 1, 2), gwd, axis=2)
    ls = min(s_len, SG_CHUNK)
    assert ls == s_len
    sg_wv = jnp.repeat(jnp.transpose(sg_w[:, :, :ls, :ls], (0, 2, 3, 1)), gwd, axis=3)
    sg_wv = sg_wv.reshape(depth, ls * ls, bw)
    sg_bv = jnp.repeat(jnp.swapaxes(sg_b[:, :, :ls], 1, 2), gwd, axis=2)
    bias_rows = jnp.broadcast_to(jnp.tile(sb_bias, (1, s_len))[:, :, None],
                                 (depth, s_len * n_heads, KEY_BLOCK))

    bp = -(-n_seq // SUBLANES) * SUBLANES
    c_all = jnp.concatenate([c_sample, c_prompt, jnp.zeros((bp - n_seq, d), F32)], axis=0)
    mod_s, mod_p = _adaln(c_all, w_ada, b_ada, nb, bp)
    mod_p = mod_p.reshape(depth, bp, 1, N_MOD * d)

    xp = x_prompt.reshape(n_seq * t_len, d)
    xs = x_sample.reshape(nb, s_len * d)
    ck = jnp.transpose(cache_k, (0, 1, 3, 4, 2))
    cv = jnp.transpose(cache_v, (0, 1, 3, 4, 2))
    sconv = state_conv.reshape(depth, nb, -1)
    spool = state_pool.reshape(depth, nb, -1)

    def prompt_group(want):
        tm = _row_tile(t_len, want)
        return _Group("prompt", n_seq * t_len // tm, tm, d, t_len // tm)

    sample_group = _Group("sample", s_len, nb, d)
    tf = 512

    outs = {k: [] for k in ("cp", "pp", "cs", "ps", "sg")}
    ktp = vtp = kts = vts = None
    for l in range(depth):
        last = l == depth - 1
        q, k, v, ktp, vtp, rest = _in_proj(prompt_group(512), l, xp, mod_p, norm_mix3, w_in_b, bw, n_proj,
                                           q_scale, BF16, (depth, n_seq, bw, t_len), ktp, vtp)
        ya = _attn_prompt(q.reshape(n_seq, t_len, bw), k.reshape(n_seq, t_len, bw),
                          v.reshape(n_seq, t_len, bw), sb_bias[l], hd)
        gm = prompt_group(256)
        xp, cst, pst = _mixer_prompt(gm, l, xp, ya.reshape(n_seq * t_len, bw), rest, mod_p, norm_mix3,
                                     w_gate, conv_w, pool_w_b, pool_scale3, sg_w, sg_bt, sg_norm3,
                                     w_branch_b, w_o_b, bw, n_seq)
        xp = _ffn(prompt_group(1024), l, xp, mod_p, norm_ffn3, w_ff1_b, w_ff2_b, norm_final2, last, tf)
        outs["cp"].append(cst[:, cst.shape[1] - (CONV_TAPS - 1):])
        outs["pp"].append(pst[:, pst.shape[1] - POOL_PAD:])

        q, k, v, kts, vts, rest = _in_proj(sample_group, l, xs, mod_s, norm_mix3, w_in_b, bw, n_proj,
                                           q_scale, F32, (depth, s_len, bw, nb), kts, vts)
        ya = _attn_decode(l, q.reshape(nb, s_len, bw), k.reshape(nb, s_len, bw),
                          v.reshape(nb, s_len, bw), ck, cv, page_table, bias_rows, hd)
        xs, cst, pst, sgv = _mixer_decode(l, xs, ya.reshape(nb, s_len * bw), rest, sconv, spool,
                                          mod_s, norm_mix3, w_gate, conv_w, pool_w_b, pool_scale3,
                                          sg_wv, sg_bv, sg_norm3, w_branch_b, w_o_b, bw, d, s_len,
                                          past_len, min(nb, 32))
        xs = _ffn(sample_group, l, xs, mod_s, norm_ffn3, w_ff1_b, w_ff2_b, norm_final2, last, tf)
        outs["cs"].append(cst.reshape(nb, CONV_TAPS - 1, bw))
        outs["ps"].append(pst.reshape(nb, POOL_PAD, bw))
        outs["sg"].append(sgv.reshape(nb, s_len, bw))

    st = {k: jnp.stack(v) for k, v in outs.items()}

    def prompt_state(a):
        return jnp.transpose(a.reshape(depth, n_seq, n_heads, hd, t_len), (0, 1, 4, 2, 3))

    def sample_state(a):
        return jnp.transpose(a.reshape(depth, s_len, n_heads, hd, nb), (0, 4, 1, 2, 3))

    return (xp.reshape(n_seq, t_len, d), xs.reshape(nb, s_len, d), prompt_state(ktp), prompt_state(vtp),
            st["cp"], st["pp"], sample_state(kts), sample_state(vts), st["cs"], st["ps"], st["sg"])
```

```python
import functools

import jax
import jax.numpy as jnp
from jax import lax
from jax.experimental import pallas as pl
from jax.experimental.pallas import tpu as pltpu

F32 = jnp.float32
BF16 = jnp.bfloat16

NORM_EPS = 1e-6
CONV_TAPS = 3
POOL_WINDOWS = (2, 4, 8, 16)
POOL_PAD = max(POOL_WINDOWS) - 1
SG_CHUNK = 128
N_MOD = 6
N_BRANCH = 4

LANES = 128
SUBLANES = 8
KEY_BLOCK = 128
Q_TILE = 256
ATTN_HEAD_PAIRS = 1
VMEM_LIMIT_BYTES = 56 * 1024 * 1024


def _params(n_axes):
    return pltpu.CompilerParams(
        dimension_semantics=("arbitrary",) * n_axes, vmem_limit_bytes=VMEM_LIMIT_BYTES)


def _dot(a, b):
    return jnp.dot(a, b, preferred_element_type=F32)


def _dot_nt(a, b):
    return lax.dot_general(a, b, (((1,), (1,)), ((), ())), preferred_element_type=F32)


def _sigmoid(x):
    return 1.0 / (1.0 + jnp.exp(-x))


def _rms(x, g):
    return x * lax.rsqrt(jnp.mean(x * x, axis=-1, keepdims=True) + NORM_EPS) * g


def _norm_mod(x, g, sc, sh):
    return _rms(x, g) * (1.0 + sc) + sh


def _adaln_kernel(c_ref, w_ref, b_ref, os_ref, op_ref):
    c = c_ref[...]
    a = (c * _sigmoid(c)).astype(BF16)
    r = _dot(a, w_ref[...].astype(BF16)) + b_ref[...]
    ns = os_ref.shape[0]
    os_ref[...] = r[:ns]
    op_ref[...] = r[ns:]


def _adaln(c_all, w_ada, b_ada, n_sample, n_prompt_pad):
    depth, d, n = w_ada.shape
    tn = 1024
    rows = c_all.shape[0]
    return pl.pallas_call(
        _adaln_kernel,
        grid=(depth, n // tn),
        in_specs=[
            pl.BlockSpec((rows, d), lambda l, j: (0, 0)),
            pl.BlockSpec((None, d, tn), lambda l, j: (l, 0, j)),
            pl.BlockSpec((None, 1, tn), lambda l, j: (l, 0, j)),
        ],
        out_specs=[
            pl.BlockSpec((None, n_sample, tn), lambda l, j: (l, 0, j)),
            pl.BlockSpec((None, n_prompt_pad, tn), lambda l, j: (l, 0, j)),
        ],
        out_shape=[
            jax.ShapeDtypeStruct((depth, n_sample, n), F32),
            jax.ShapeDtypeStruct((depth, n_prompt_pad, n), F32),
        ],
        compiler_params=_params(2),
        name="adaln",
    )(c_all, w_ada, b_ada.reshape(depth, 1, n))


class _Group:
    def __init__(self, kind, n_tiles, tm, d, tiles_per_seq=None):
        self.kind, self.n_tiles, self.tm, self.d, self.tps = kind, n_tiles, tm, d, tiles_per_seq

    def x_spec(self, width, col_blocks=1, col=0):
        if self.kind == "prompt":
            return pl.BlockSpec((self.tm, width), lambda i, *_: (i, col))
        return pl.BlockSpec((self.tm, width), lambda i, *_: (0, i * col_blocks + col))

    def x_spec_j(self, width, col_blocks):
        if self.kind == "prompt":
            return pl.BlockSpec((self.tm, width), lambda i, j: (i, j))
        return pl.BlockSpec((self.tm, width), lambda i, j: (0, i * col_blocks + j))

    def mod_spec(self, layer, chunk):
        if self.kind == "prompt":
            tps = self.tps
            return pl.BlockSpec((None, None, 1, self.d), lambda i, *_: (layer, i // tps, 0, chunk))
        return pl.BlockSpec((None, self.tm, self.d), lambda i, *_: (layer, 0, chunk))


def _const_spec(shape):
    nd = len(shape)
    return pl.BlockSpec(shape, lambda *_: (0,) * nd)


def _layer_spec(shape, layer):
    nd = len(shape)
    return pl.BlockSpec((None,) + tuple(shape), lambda *_: (layer,) + (0,) * nd)


def _in_proj_kernel(x_ref, sh_ref, sc_ref, g_ref, w_ref, *rest, q_scale):
    q_ref, k_ref, v_ref, kt_ref, vt_ref, r_ref = rest[-6:]
    h = _norm_mod(x_ref[...], g_ref[...], sc_ref[...], sh_ref[...]).astype(BF16)
    bw = q_ref.shape[1]
    p = _dot(h, w_ref[:, :3 * bw])
    k, v = p[:, bw:2 * bw], p[:, 2 * bw:3 * bw]
    q_ref[...] = (p[:, :bw] * q_scale).astype(q_ref.dtype)
    k_ref[...] = k.astype(k_ref.dtype)
    v_ref[...] = v.astype(v_ref.dtype)
    kt_ref[...] = k.T
    vt_ref[...] = v.T
    r_ref[...] = _dot(h, w_ref[:, 3 * bw:])


def _in_proj(grp, layer, x, mod, norm_w, w_in, bw, n_cols, q_scale, kv_dtype, state_shape, state_kt,
             state_vt):
    d = grp.d
    rows, cols = x.shape
    n_pos = cols // d
    n_rest = n_cols - 3 * bw
    o_spec = grp.x_spec(bw)
    if grp.kind == "prompt":
        tps = grp.tps
        t_spec = pl.BlockSpec((None, None, bw, grp.tm), lambda i: (layer, i // tps, 0, i % tps))
    else:
        t_spec = pl.BlockSpec((None, None, bw, grp.tm), lambda i: (layer, i, 0, 0))
    carried = [] if state_kt is None else [state_kt, state_vt]
    n_in = 5
    return pl.pallas_call(
        functools.partial(_in_proj_kernel, q_scale=q_scale),
        grid=(grp.n_tiles,),
        in_specs=[
            grp.x_spec(d),
            grp.mod_spec(layer, 0),
            grp.mod_spec(layer, 1),
            _layer_spec((1, d), layer),
            _layer_spec((d, n_cols), layer),
        ] + [pl.BlockSpec(memory_space=pl.ANY)] * len(carried),
        out_specs=[o_spec, o_spec, o_spec, t_spec, t_spec, grp.x_spec(n_rest)],
        out_shape=[
            jax.ShapeDtypeStruct((rows, n_pos * bw), BF16),
            jax.ShapeDtypeStruct((rows, n_pos * bw), kv_dtype),
            jax.ShapeDtypeStruct((rows, n_pos * bw), kv_dtype),
            jax.ShapeDtypeStruct(state_shape, F32),
            jax.ShapeDtypeStruct(state_shape, F32),
            jax.ShapeDtypeStruct((rows, n_pos * n_rest), F32),
        ],
        input_output_aliases={n_in: 3, n_in + 1: 4} if carried else {},
        compiler_params=_params(1),
        name="in_proj",
    )(x, mod, mod, norm_w, w_in, *carried)


def _suffix_prod_sublanes(x):
    sub = lax.broadcasted_iota(jnp.int32, x.shape, 0)
    y = x
    for k in (1, 2, 4):
        shifted = pltpu.roll(y, SUBLANES - k, axis=0)
        y = y * jnp.where(sub + k < SUBLANES, shifted, 1.0)
    return y


def _shift_up_sublanes(y):
    sub = lax.broadcasted_iota(jnp.int32, y.shape, 0)
    return jnp.where(sub + 1 < SUBLANES, pltpu.roll(y, SUBLANES - 1, axis=0), 1.0)


def _sb_tile_permuted(s_tile, carry, diag_offset):
    n = s_tile.shape[1]
    if n > LANES:
        parts = [_sb_tile_permuted(s_tile[:, c:c + LANES], carry[:, c:c + LANES],
                                   None if diag_offset is None else diag_offset - c)
                 for c in range(0, n, LANES)]
        return (jnp.concatenate([p[0] for p in parts], axis=1),
                jnp.concatenate([p[1] for p in parts], axis=1))
    if diag_offset is not None and diag_offset >= n:
        return jnp.zeros(s_tile.shape, BF16), carry
    rest = 0.5 - 0.5 * jnp.tanh(s_tile)
    if diag_offset is not None:
        r = lax.broadcasted_iota(jnp.int32, s_tile.shape, 0)
        q_idx = lax.broadcasted_iota(jnp.int32, s_tile.shape, 1)
        k_idx = (r & 7) * 16 + (r >> 3) + diag_offset
        rest = jnp.where(k_idx < q_idx, rest, 1.0)
    nv = KEY_BLOCK // SUBLANES
    taken = [None] * nv
    acc = None
    for v in range(nv - 1, -1, -1):
        nxt = rest[v * SUBLANES:(v + 1) * SUBLANES]
        nxt = nxt if acc is None else acc * nxt
        taken[v] = (1.0 - nxt) if acc is None else (acc - nxt)
        acc = nxt
    incl = _suffix_prod_sublanes(acc)
    base = _shift_up_sublanes(incl) * carry
    w = jnp.concatenate([t * base for t in taken], axis=0).astype(BF16)
    return w, carry * incl[0:1]


def _attn_prompt_kernel(bias_ref, q_ref, k_ref, v_ref, o_ref, kp_scr, vt_scr, qt_scr, s_a, s_b, w_a, w_b,
                        acc_scr, *, hd, n_pairs):
    hp = pl.program_id(1)
    i = pl.program_id(2)
    tq = q_ref.shape[0]
    n_chunks = k_ref.shape[0] // tq
    halves = tq // KEY_BLOCK

    @pl.when(i == 0)
    def _():
        r = lax.broadcasted_iota(jnp.int32, (KEY_BLOCK, KEY_BLOCK), 0)
        j = lax.broadcasted_iota(jnp.int32, (KEY_BLOCK, KEY_BLOCK), 1)
        perm = jnp.where(j == (r & 7) * 16 + (r >> 3), 1.0, 0.0).astype(BF16)
        ones0 = jnp.where((j >= hd) & (j < hd + 2), 1.0, 0.0)
        ones1 = jnp.where(j < 2, 1.0, 0.0)

        def body(c, carry):
            for hf in range(halves):
                rows = pl.ds(pl.multiple_of(c * tq + hf * KEY_BLOCK, KEY_BLOCK), KEY_BLOCK)
                dst = slice(hf * KEY_BLOCK, (hf + 1) * KEY_BLOCK)
                for p in range(n_pairs):
                    cols = slice(p * LANES, (p + 1) * LANES)
                    kp = _dot(perm, k_ref[rows, cols])
                    vp = _dot(perm, v_ref[rows, cols])
                    kp_scr[2 * p, c, dst, :] = jnp.where(j < hd, kp, ones0).astype(BF16)
                    kp_scr[2 * p + 1, c, dst, :] = jnp.where(j >= hd, kp, ones1).astype(BF16)
                    vt_scr[p, c, :, dst] = vp.T.astype(BF16)
            return carry

        lax.fori_loop(0, n_chunks, body, 0)

    lane = lax.broadcasted_iota(jnp.int32, (tq, LANES), 1)
    for p in range(n_pairs):
        q = q_ref[:, p * LANES:(p + 1) * LANES].astype(F32)
        for h in range(2):
            b = jnp.full(q.shape, 0.5 * bias_ref[2 * (hp * n_pairs + p) + h], F32)
            b_hi = b.astype(BF16).astype(F32)
            b_lo = b - b_hi
            own = (lane < hd) if h == 0 else (lane >= hd)
            first = hd if h == 0 else 0
            qm = jnp.where(own, q, jnp.where(lane == first, b_hi, jnp.where(lane == first + 1, b_lo, 0.0)))
            qt_scr[2 * p + h] = qm.T.astype(BF16)
    acc_scr[...] = jnp.zeros_like(acc_scr)

    n_heads = 2 * n_pairs

    def scores(k, s_buf):
        c = jnp.maximum(i - k, 0)
        for h in range(n_heads):
            s_buf[h] = _dot(kp_scr[h, c], qt_scr[h])

    def scan(s_buf, w_buf, diag, carry):
        out = []
        for h in range(n_heads):
            cr = carry[h]
            for hf in range(halves - 1, -1, -1):
                rows = slice(hf * KEY_BLOCK, (hf + 1) * KEY_BLOCK)
                w, cr = _sb_tile_permuted(s_buf[h, rows, :], cr,
                                          hf * KEY_BLOCK if diag else None)
                w_buf[h, rows, :] = w
            out.append(cr)
        return tuple(out)

    def values(k, w_buf):
        for h in range(n_heads):
            acc_scr[h] += _dot(vt_scr[h // 2, i - k], w_buf[h])

    n = i + 1
    ones = jnp.ones((1, tq), F32)
    scores(0, s_a)
    scores(1, s_b)
    carry = scan(s_a, w_a, True, (ones,) * n_heads)

    def pair(m, cr):
        k = 2 * m
        scores(k, s_a)
        cr = scan(s_b, w_b, False, cr)
        values(k - 2, w_a)
        scores(k + 1, s_b)
        cr = scan(s_a, w_a, False, cr)
        values(k - 1, w_b)
        return cr

    carry = lax.fori_loop(1, (n - 1) // 2 + 1, pair, carry)

    @pl.when(n % 2 == 0)
    def _():
        scan(s_b, w_b, False, carry)
        values(n - 2, w_a)
        values(n - 1, w_b)

    @pl.when(n % 2 == 1)
    def _():
        values(n - 1, w_a)

    row = lax.broadcasted_iota(jnp.int32, (2 * hd, tq), 0)
    for p in range(n_pairs):
        res = jnp.where(row < hd, acc_scr[2 * p], acc_scr[2 * p + 1])
        o_ref[:, p * LANES:(p + 1) * LANES] = res.T.astype(o_ref.dtype)


def _attn_prompt(q, k, v, bias, hd):
    b, t, c = q.shape
    n_pairs = ATTN_HEAD_PAIRS
    width = n_pairs * LANES
    assert 2 * hd == LANES and t % Q_TILE == 0 and c % width == 0
    n_chunks = t // Q_TILE
    nh = 2 * n_pairs
    qo_spec = pl.BlockSpec((None, Q_TILE, width), lambda bi, hp, i: (bi, i, hp))
    kv_spec = pl.BlockSpec((None, t, width), lambda bi, hp, i: (bi, 0, hp))
    return pl.pallas_call(
        functools.partial(_attn_prompt_kernel, hd=hd, n_pairs=n_pairs),
        grid=(b, c // width, n_chunks),
        in_specs=[pl.BlockSpec(memory_space=pltpu.SMEM), qo_spec, kv_spec, kv_spec],
        out_specs=qo_spec,
        out_shape=jax.ShapeDtypeStruct((b, t, c), BF16),
        scratch_shapes=[
            pltpu.VMEM((nh, n_chunks, Q_TILE, LANES), BF16),
            pltpu.VMEM((n_pairs, n_chunks, LANES, Q_TILE), BF16),
            pltpu.VMEM((nh, LANES, Q_TILE), BF16),
            pltpu.VMEM((nh, Q_TILE, Q_TILE), F32),
            pltpu.VMEM((nh, Q_TILE, Q_TILE), F32),
            pltpu.VMEM((nh, Q_TILE, Q_TILE), BF16),
            pltpu.VMEM((nh, Q_TILE, Q_TILE), BF16),
            pltpu.VMEM((nh, LANES, Q_TILE), F32),
        ],
        compiler_params=_params(3),
        name="attn_prompt",
    )(bias, q, k, v)


def _suffix_prod_lanes(x):
    n = x.shape[1]
    lane = lax.broadcasted_iota(jnp.int32, x.shape, 1)
    y = x
    k = 1
    while k < n:
        shifted = pltpu.roll(y, n - k, axis=1)
        y = y * jnp.where(lane + k < n, shifted, 1.0)
        k *= 2
    return y


def _attn_decode_kernel(pt_ref, bias_ref, q_ref, kn_ref, vn_ref, *rest, hd, n_pages):
    del pt_ref
    k_pages = rest[:n_pages]
    v_pages = rest[n_pages:2 * n_pages]
    o_ref = rest[2 * n_pages]
    s_len, c = q_ref.shape
    n_heads = c // hd
    nrow = s_len * n_heads

    head_row = lax.broadcasted_iota(jnp.int32, (n_heads, c), 0)
    lane = lax.broadcasted_iota(jnp.int32, (n_heads, c), 1)
    head_mask = (lane >= head_row * hd) & (lane < (head_row + 1) * hd)

    q = q_ref[...].astype(F32)
    q_rows = jnp.concatenate(
        [jnp.where(head_mask, jnp.broadcast_to(q[t:t + 1], (n_heads, c)), 0.0) for t in range(s_len)],
        axis=0).astype(BF16)
    bias_half = 0.5 * bias_ref[...]

    pad = jnp.zeros((KEY_BLOCK - s_len, c), F32)
    k_new = jnp.concatenate([kn_ref[...], pad], axis=0).astype(BF16)
    v_new = jnp.concatenate([vn_ref[...], pad], axis=0).astype(BF16)
    row = lax.broadcasted_iota(jnp.int32, (nrow, KEY_BLOCK), 0)
    key = lax.broadcasted_iota(jnp.int32, (nrow, KEY_BLOCK), 1)
    own_valid = key * n_heads + n_heads <= row

    pages = range(n_pages - 1, -1, -1)
    def rest_of(s):
        return 0.5 - 0.5 * jnp.tanh(s + bias_half)

    rest_all = jnp.concatenate(
        [jnp.where(own_valid, rest_of(_dot_nt(q_rows, k_new)), 1.0)]
        + [rest_of(_dot(q_rows, k_pages[p][...].reshape(c, KEY_BLOCK).astype(BF16))) for p in pages],
        axis=0)
    incl = _suffix_prod_lanes(rest_all)
    lane = lax.broadcasted_iota(jnp.int32, incl.shape, 1)
    later = jnp.where(lane + 1 < KEY_BLOCK, pltpu.roll(incl, KEY_BLOCK - 1, axis=1), 1.0)
    taken = later - incl
    total = jnp.broadcast_to(incl[:, 0:1], incl.shape)
    carry = None
    acc = None
    for j in range(n_pages + 1):
        rows = slice(j * nrow, (j + 1) * nrow)
        w = (taken[rows] if carry is None else taken[rows] * carry).astype(BF16)
        carry = total[rows] if carry is None else carry * total[rows]
        if j == 0:
            acc = _dot(w, v_new)
        else:
            acc = acc + _dot_nt(w, v_pages[n_pages - j][...].reshape(c, KEY_BLOCK).astype(BF16))

    outs = [jnp.sum(jnp.where(head_mask, acc[t * n_heads:(t + 1) * n_heads], 0.0), axis=0, keepdims=True)
            for t in range(s_len)]
    o_ref[...] = jnp.concatenate(outs, axis=0)


def _attn_decode(layer, q, k_new, v_new, cache_kt, cache_vt, page_table, bias_rows, hd):
    nb, s_len, c = q.shape
    n_pages = page_table.shape[1]
    n_heads, page = cache_kt.shape[2], cache_kt.shape[4]
    nrow = s_len * n_heads
    assert page == KEY_BLOCK and n_heads == SUBLANES and n_heads * hd == c
    seq_spec = pl.BlockSpec((None, s_len, c), lambda b, pt: (b, 0, 0))

    def page_spec(p):
        return pl.BlockSpec((None, None, n_heads, hd, page), lambda b, pt: (layer, pt[b, p], 0, 0, 0))

    grid_spec = pltpu.PrefetchScalarGridSpec(
        num_scalar_prefetch=1,
        grid=(nb,),
        in_specs=[pl.BlockSpec((None, nrow, KEY_BLOCK), lambda b, pt: (layer, 0, 0)),
                  seq_spec, seq_spec, seq_spec]
        + [page_spec(p) for p in range(n_pages)] + [page_spec(p) for p in range(n_pages)],
        out_specs=seq_spec,
    )
    return pl.pallas_call(
        functools.partial(_attn_decode_kernel, hd=hd, n_pages=n_pages),
        grid_spec=grid_spec,
        out_shape=jax.ShapeDtypeStruct((nb, s_len, c), F32),
        compiler_params=_params(1),
        name="attn_decode",
    )(page_table, bias_rows, q, k_new, v_new, *([cache_kt] * n_pages), *([cache_vt] * n_pages))


def _merge(x, ys, gates, wb_ref, wo_ref, g1):
    merged = None
    for i, (y, gate) in enumerate(zip(ys, gates)):
        term = _sigmoid(gate) * _dot(y.astype(BF16), wb_ref[i])
        merged = term if merged is None else merged + term
    return x + g1 * _dot(merged.astype(BF16), wo_ref[...])


def _pool_count(pos, win):
    return jnp.minimum(pos + 1.0, float(win))


def _branch_gates(x, nw_ref, sc_ref, sh_ref, wg_ref, d):
    h = _norm_mod(x, nw_ref[...], sc_ref[...], sh_ref[...]).astype(BF16)
    return [_dot(h, wg_ref[:, i * d:(i + 1) * d]) for i in range(N_BRANCH)]


def _mixer_prompt_kernel(x_ref, ya_ref, cv_ref, cvh_ref, pi_ref, pih_ref, su_ref, sv_ref,
                         sh_ref, sc_ref, nw_ref, wg_ref, cw_ref, pw_ref, ps_ref, sw_ref, sb_ref,
                         sn_ref, wb_ref, wo_ref, m1_ref, o_ref, cst_ref, pst_ref, cbuf, pbuf,
                         *, bw, tiles_per_seq):
    tm = x_ref.shape[0]
    first = (pl.program_id(0) % tiles_per_seq) == 0
    hc = cvh_ref.shape[0]
    hp = pih_ref.shape[0]

    cv = cv_ref[...]
    cx = cv[:, 2 * bw:3 * bw] * cv[:, 0:bw]
    cvh = cvh_ref[...]
    cbuf[0:hc, :] = jnp.where(first, 0.0, cvh[:, 2 * bw:3 * bw] * cvh[:, 0:bw])
    cbuf[hc:hc + tm, :] = cx
    cw = cw_ref[...]
    conv = cw[0:1] * cbuf[pl.ds(hc - 2, tm), :] + cw[1:2] * cbuf[pl.ds(hc - 1, tm), :] + cw[2:3] * cx
    y_b = cv[:, bw:2 * bw] * conv
    cst_ref[...] = cx[tm - hc:tm]

    pin = pi_ref[...]
    pbuf[0:hp, :] = jnp.where(first, 0.0, pih_ref[...])
    pbuf[hp:hp + tm, :] = pin
    pst_ref[...] = pin[tm - hp:tm]
    pos = (lax.broadcasted_iota(jnp.int32, (tm, 1), 0)
           + (pl.program_id(0) % tiles_per_seq) * tm).astype(F32)
    gw = bw // len(POOL_WINDOWS)
    y_c = []
    for g, win in enumerate(POOL_WINDOWS):
        lanes = slice(g * gw, (g + 1) * gw)
        wsum = pin[:, lanes]
        for kk in range(1, win):
            wsum = wsum + pbuf[pl.ds(hp - kk, tm), lanes]
        diff = wsum / _pool_count(pos, win) - pin[:, lanes]
        y_c.append(_dot(diff.astype(BF16), pw_ref[g]))
    y_c = jnp.concatenate(y_c, axis=1) * ps_ref[...]

    vn = _rms(sv_ref[...], sn_ref[...])
    n_groups = sw_ref.shape[0] // SG_CHUNK
    gwd = bw // n_groups
    r = lax.broadcasted_iota(jnp.int32, sw_ref.shape, 0) & (SG_CHUNK - 1)
    s_idx = lax.broadcasted_iota(jnp.int32, sw_ref.shape, 1)
    w_tril = jnp.where(s_idx <= r, sw_ref[...], 0.0).astype(BF16)
    lane_grp = lax.broadcasted_iota(jnp.int32, (SG_CHUNK, bw), 1)
    mixed = []
    for n in range(tm // SG_CHUNK):
        full = _dot(w_tril, vn[n * SG_CHUNK:(n + 1) * SG_CHUNK].astype(BF16))
        m = None
        for g in range(n_groups):
            blk = full[g * SG_CHUNK:(g + 1) * SG_CHUNK]
            sel = jnp.where((lane_grp >= g * gwd) & (lane_grp < (g + 1) * gwd), blk, 0.0)
            m = sel if m is None else m + sel
        mixed.append(m + sb_ref[...])
    y_d = su_ref[...] * jnp.concatenate(mixed, axis=0)

    x = x_ref[...]
    gates = _branch_gates(x, nw_ref, sc_ref, sh_ref, wg_ref, x.shape[1])
    o_ref[...] = _merge(x, (ya_ref[...], y_b, y_c, y_d), gates, wb_ref, wo_ref, m1_ref[...])


def _mixer_prompt(grp, layer, x, ya, rest, mod, norm_w, w_gate, conv_w, pool_w, pool_scale, sg_w, sg_bt,
                  sg_norm, w_branch, w_o, bw, n_seq):
    d, tm, tps = grp.d, grp.tm, grp.tps
    rows = x.shape[0]
    hc, hp = SUBLANES, 2 * SUBLANES
    assert hc >= CONV_TAPS - 1 and hp >= POOL_PAD and tm % SG_CHUNK == 0

    def halo(h, width, col):
        per = tm // h
        return pl.BlockSpec((h, width), lambda i: (jnp.maximum(i * per - 1, 0), col))

    n_groups = sg_w.shape[1]
    return pl.pallas_call(
        functools.partial(_mixer_prompt_kernel, bw=bw, tiles_per_seq=tps),
        grid=(grp.n_tiles,),
        in_specs=[
            grp.x_spec(d), grp.x_spec(bw),
            grp.x_spec(3 * bw, col=0), halo(hc, 3 * bw, 0),
            grp.x_spec(bw, col=3), halo(hp, bw, 3),
            grp.x_spec(bw, col=4), grp.x_spec(bw, col=5),
            grp.mod_spec(layer, 0), grp.mod_spec(layer, 1),
            _layer_spec((1, d), layer),
            _layer_spec(w_gate.shape[1:], layer),
            _layer_spec((CONV_TAPS, bw), layer),
            _layer_spec(pool_w.shape[1:], layer),
            _layer_spec((1, bw), layer),
            _layer_spec((n_groups * SG_CHUNK, SG_CHUNK), layer),
            _layer_spec((SG_CHUNK, bw), layer),
            _layer_spec((1, bw), layer),
            _layer_spec(w_branch.shape[1:], layer),
            _layer_spec(w_o.shape[1:], layer),
            grp.mod_spec(layer, 2),
        ],
        out_specs=[
            grp.x_spec(d),
            pl.BlockSpec((None, hc, bw), lambda i: (i // tps, 0, 0)),
            pl.BlockSpec((None, hp, bw), lambda i: (i // tps, 0, 0)),
        ],
        out_shape=[
            jax.ShapeDtypeStruct((rows, d), F32),
            jax.ShapeDtypeStruct((n_seq, hc, bw), F32),
            jax.ShapeDtypeStruct((n_seq, hp, bw), F32),
        ],
        scratch_shapes=[pltpu.VMEM((hc + tm, bw), F32), pltpu.VMEM((hp + tm, bw), F32)],
        compiler_params=_params(1),
        name="mixer_prompt",
    )(x, ya, rest, rest, rest, rest, rest, rest, mod, mod, norm_w, w_gate,
      conv_w, pool_w, pool_scale, sg_w.reshape(sg_w.shape[0], n_groups * SG_CHUNK, SG_CHUNK),
      sg_bt, sg_norm, w_branch, w_o, mod)


def _mixer_decode_kernel(x_ref, ya_ref, r_ref, cs_ref, ps_ref, sh_ref, sc_ref, nw_ref, wg_ref,
                         cw_ref, pw_ref, psc_ref, swv_ref, sbv_ref, sn_ref, wb_ref, wo_ref, m1_ref,
                         o_ref, cst_ref, pst_ref, sgv_ref, *, bw, d, s_len, pos0):
    n_rest = r_ref.shape[1] // s_len

    def rest(t, off, width):
        return r_ref[:, t * n_rest + off:t * n_rest + off + width]

    cw = cw_ref[...]
    cx = [rest(t, 2 * bw, bw) * rest(t, 0, bw) for t in range(s_len)]
    taps = CONV_TAPS - 1
    xp = [cs_ref[:, i * bw:(i + 1) * bw] for i in range(taps)] + cx
    y_b = []
    for t in range(s_len):
        conv = cw[0:1] * xp[t]
        for i in range(1, CONV_TAPS):
            conv = conv + cw[i:i + 1] * xp[t + i]
        y_b.append(rest(t, bw, bw) * conv)
    for i in range(taps):
        cst_ref[:, i * bw:(i + 1) * bw] = xp[len(xp) - taps + i]

    pin = [rest(t, 3 * bw, bw) for t in range(s_len)]
    pp = [ps_ref[:, i * bw:(i + 1) * bw] for i in range(POOL_PAD)] + pin
    for i in range(POOL_PAD):
        pst_ref[:, i * bw:(i + 1) * bw] = pp[len(pp) - POOL_PAD + i]
    gw = bw // len(POOL_WINDOWS)
    diffs = [[] for _ in POOL_WINDOWS]
    for t in range(s_len):
        for g, win in enumerate(POOL_WINDOWS):
            lanes = slice(g * gw, (g + 1) * gw)
            wsum = pp[POOL_PAD + t][:, lanes]
            for kk in range(1, win):
                wsum = wsum + pp[POOL_PAD + t - kk][:, lanes]
            cnt = float(min(pos0 + t + 1, win))
            diffs[g].append(wsum / cnt - pin[t][:, lanes])
    y_c_groups = [_dot(jnp.concatenate(diffs[g], axis=0).astype(BF16), pw_ref[g])
                  for g in range(len(POOL_WINDOWS))]
    y_c_all = jnp.concatenate(y_c_groups, axis=1) * psc_ref[...]

    tb = x_ref.shape[0]
    vn = [_rms(rest(t, 5 * bw, bw), sn_ref[...]) for t in range(s_len)]
    y_d = []
    for t in range(s_len):
        sgv_ref[:, t * bw:(t + 1) * bw] = vn[t]
        mixed = sbv_ref[t:t + 1]
        for s in range(t + 1):
            mixed = mixed + swv_ref[t * s_len + s:t * s_len + s + 1] * vn[s]
        y_d.append(rest(t, 4 * bw, bw) * mixed)

    cat = lambda parts: jnp.concatenate(parts, axis=0)
    x_all = cat([x_ref[:, t * d:(t + 1) * d] for t in range(s_len)])
    ya_all = cat([ya_ref[:, t * bw:(t + 1) * bw] for t in range(s_len)])
    rep = lambda ref: cat([ref[...]] * s_len)
    h_all = _norm_mod(x_all, nw_ref[...], rep(sc_ref), rep(sh_ref)).astype(BF16)
    gates = [_dot(h_all, wg_ref[:, i * d:(i + 1) * d]) for i in range(N_BRANCH)]
    g1 = rep(m1_ref)
    out = _merge(x_all, (ya_all, cat(y_b), y_c_all, cat(y_d)), gates, wb_ref, wo_ref, g1)
    for t in range(s_len):
        o_ref[:, t * d:(t + 1) * d] = out[t * tb:(t + 1) * tb]


def _mixer_decode(layer, x, ya, rest, state_conv, state_pool, mod, norm_w, w_gate, conv_w, pool_w,
                  pool_scale, sg_wv, sg_bv, sg_norm, w_branch, w_o, bw, d, s_len, pos0, tb):
    nb = x.shape[0]

    def row_spec(width):
        return pl.BlockSpec((tb, width), lambda i: (i, 0))

    def state_spec(width):
        return pl.BlockSpec((None, tb, width), lambda i: (layer, i, 0))

    def mod_spec(chunk):
        return pl.BlockSpec((None, tb, d), lambda i: (layer, i, chunk))

    return pl.pallas_call(
        functools.partial(_mixer_decode_kernel, bw=bw, d=d, s_len=s_len, pos0=pos0),
        grid=(nb // tb,),
        in_specs=[
            row_spec(s_len * d), row_spec(s_len * bw), row_spec(rest.shape[1]),
            state_spec(state_conv.shape[2]), state_spec(state_pool.shape[2]),
            mod_spec(0), mod_spec(1),
            _layer_spec((1, d), layer),
            _layer_spec(w_gate.shape[1:], layer),
            _layer_spec((CONV_TAPS, bw), layer),
            _layer_spec(pool_w.shape[1:], layer),
            _layer_spec((1, bw), layer),
            _layer_spec(sg_wv.shape[1:], layer),
            _layer_spec(sg_bv.shape[1:], layer),
            _layer_spec((1, bw), layer),
            _layer_spec(w_branch.shape[1:], layer),
            _layer_spec(w_o.shape[1:], layer),
            mod_spec(2),
        ],
        out_specs=[row_spec(s_len * d), row_spec(state_conv.shape[2]), row_spec(state_pool.shape[2]),
                   row_spec(s_len * bw)],
        out_shape=[
            jax.ShapeDtypeStruct((nb, s_len * d), F32),
            jax.ShapeDtypeStruct((nb, state_conv.shape[2]), F32),
            jax.ShapeDtypeStruct((nb, state_pool.shape[2]), F32),
            jax.ShapeDtypeStruct((nb, s_len * bw), F32),
        ],
        compiler_params=_params(1),
        name="mixer_decode",
    )(x, ya, rest, state_conv, state_pool, mod, mod, norm_w, w_gate, conv_w, pool_w, pool_scale,
      sg_wv, sg_bv, sg_norm, w_branch, w_o, mod)


def _ffn_kernel(x_ref, sh_ref, sc_ref, gt_ref, g_ref, w1_ref, w2_ref, fw_ref, o_ref, *, final_norm, tf):
    x = x_ref[...]
    h = _norm_mod(x, g_ref[...], sc_ref[...], sh_ref[...]).astype(BF16)
    acc = None
    for c in range(0, w1_ref.shape[1], tf):
        a = jnp.maximum(_dot(h, w1_ref[:, c:c + tf]), 0.0)
        part = _dot((a * a).astype(BF16), w2_ref[c:c + tf, :])
        acc = part if acc is None else acc + part
    y = x + gt_ref[...] * acc
    if final_norm:
        y = _rms(y, fw_ref[...])
    o_ref[...] = y


def _ffn(grp, layer, x, mod, norm_w, w1, w2, final_w, final_norm, tf):
    d = grp.d
    dff = w1.shape[-1]
    assert dff % tf == 0
    return pl.pallas_call(
        functools.partial(_ffn_kernel, final_norm=final_norm, tf=tf),
        grid=(grp.n_tiles,),
        in_specs=[
            grp.x_spec(d),
            grp.mod_spec(layer, 3), grp.mod_spec(layer, 4), grp.mod_spec(layer, 5),
            _layer_spec((1, d), layer),
            _layer_spec((d, dff), layer),
            _layer_spec((dff, d), layer),
            _const_spec((1, d)),
        ],
        out_specs=grp.x_spec(d),
        out_shape=jax.ShapeDtypeStruct(x.shape, F32),
        compiler_params=_params(1),
        name="ffn",
    )(x, mod, mod, mod, norm_w, w1, w2, final_w)


def _row_tile(t, want):
    tm = min(t, want)
    assert t % tm == 0
    return tm


def kernel(x_prompt, x_sample, cache_k, cache_v, state_conv, state_pool, page_table, c_prompt,
           c_sample, w_ada, b_ada, norm_mix, norm_ffn, w_in, sb_bias, conv_w, pool_w, pool_scale,
           sg_w, sg_b, sg_norm, w_branch, w_o, w_ff1, w_ff2, norm_final):
    n_seq, t_len, d = x_prompt.shape
    nb, s_len, _ = x_sample.shape
    depth = w_ada.shape[0]
    n_heads, hd = cache_k.shape[3], cache_k.shape[4]
    bw = n_heads * hd
    n_pages, page = page_table.shape[1], cache_k.shape[2]
    past_len = n_pages * page
    n_groups = sg_w.shape[1]
    gwd = bw // n_groups
    q_scale = 0.5 * hd ** -0.5

    w_in_b = w_in.astype(BF16)
    n_proj = 9 * bw
    w_gate = w_in_b[:, :, n_proj:]
    w_branch_b, w_o_b = w_branch.astype(BF16), w_o.astype(BF16)
    w_ff1_b, w_ff2_b = w_ff1.astype(BF16), w_ff2.astype(BF16)
    pool_w_b = pool_w.astype(BF16)
    norm_mix3, norm_ffn3 = norm_mix[:, None, :], norm_ffn[:, None, :]
    pool_scale3, sg_norm3 = pool_scale[:, None, :], sg_norm[:, None, :]
    norm_final2 = norm_final[None, :]
    chunk = min(t_len, SG_CHUNK)
    assert chunk == SG_CHUNK and sg_w.shape[2] == SG_CHUNK
    sg_bt = jnp.repeat(jnp.swapaxes(sg_b, 1, 2), gwd, axis=2)
    ls = min(s_len, SG_CHUNK)
    assert ls == s_len
    sg_wv = jnp.repeat(jnp.transpose(sg_w[:, :, :ls, :ls], (0, 2, 3, 1)), gwd, axis=3)
    sg_wv = sg_wv.reshape(depth, ls * ls, bw)
    sg_bv = jnp.repeat(jnp.swapaxes(sg_b[:, :, :ls], 1, 2), gwd, axis=2)
    bias_rows = jnp.broadcast_to(jnp.tile(sb_bias, (1, s_len))[:, :, None],
                                 (depth, s_len * n_heads, KEY_BLOCK))

    bp = -(-n_seq // SUBLANES) * SUBLANES
    c_all = jnp.concatenate([c_sample, c_prompt, jnp.zeros((bp - n_seq, d), F32)], axis=0)
    mod_s, mod_p = _adaln(c_all, w_ada, b_ada, nb, bp)
    mod_p = mod_p.reshape(depth, bp, 1, N_MOD * d)

    xp = x_prompt.reshape(n_seq * t_len, d)
    xs = x_sample.reshape(nb, s_len * d)
    ck = jnp.transpose(cache_k, (0, 1, 3, 4, 2))
    cv = jnp.transpose(cache_v, (0, 1, 3, 4, 2))
    sconv = state_conv.reshape(depth, nb, -1)
    spool = state_pool.reshape(depth, nb, -1)

    def prompt_group(want):
        tm = _row_tile(t_len, want)
        return _Group("prompt", n_seq * t_len // tm, tm, d, t_len // tm)

    sample_group = _Group("sample", s_len, nb, d)
    tf = 1024

    outs ={k: [] for k in ("cp", "pp", "cs", "ps", "sg")}
    ktp = vtp = kts = vts = None
    for l in range(depth):
        last = l == depth - 1
        q, k, v, ktp, vtp, rest = _in_proj(prompt_group(512), l, xp, mod_p, norm_mix3, w_in_b, bw, n_proj,
                                           q_scale, BF16, (depth, n_seq, bw, t_len), ktp, vtp)
        ya = _attn_prompt(q.reshape(n_seq, t_len, bw), k.reshape(n_seq, t_len, bw),
                          v.reshape(n_seq, t_len, bw), sb_bias[l], hd)
        gm = prompt_group(256)
        xp, cst, pst = _mixer_prompt(gm, l, xp, ya.reshape(n_seq * t_len, bw), rest, mod_p, norm_mix3,
                                     w_gate, conv_w, pool_w_b, pool_scale3, sg_w, sg_bt, sg_norm3,
                                     w_branch_b, w_o_b, bw, n_seq)
        xp = _ffn(prompt_group(512), l, xp, mod_p, norm_ffn3, w_ff1_b, w_ff2_b, norm_final2, last, tf)
        outs["cp"].append(cst[:, cst.shape[1] - (CONV_TAPS - 1):])
        outs["pp"].append(pst[:, pst.shape[1] - POOL_PAD:])

        q, k, v, kts, vts, rest = _in_proj(sample_group, l, xs, mod_s, norm_mix3, w_in_b, bw, n_proj,
                                           q_scale, F32, (depth, s_len, bw, nb), kts, vts)
        ya = _attn_decode(l, q.reshape(nb, s_len, bw), k.reshape(nb, s_len, bw),
                          v.reshape(nb, s_len, bw), ck, cv, page_table, bias_rows, hd)
        xs, cst, pst, sgv = _mixer_decode(l, xs, ya.reshape(nb, s_len * bw), rest, sconv, spool,
                                          mod_s, norm_mix3, w_gate, conv_w, pool_w_b, pool_scale3,
                                          sg_wv, sg_bv, sg_norm3, w_branch_b, w_o_b, bw, d, s_len,
                                          past_len, min(nb, 32))
        xs = _ffn(sample_group, l, xs, mod_s, norm_ffn3, w_ff1_b, w_ff2_b, norm_final2, last, tf)
        outs["cs"].append(cst.reshape(nb, CONV_TAPS - 1, bw))
        outs["ps"].append(pst.reshape(nb, POOL_PAD, bw))
        outs["sg"].append(sgv.reshape(nb, s_len, bw))

    st = {k: jnp.stack(v) for k, v in outs.items()}

    def prompt_state(a):
        return jnp.transpose(a.reshape(depth, n_seq, n_heads, hd, t_len), (0, 1, 4, 2, 3))

    def sample_state(a):
        return jnp.transpose(a.reshape(depth, s_len, n_heads, hd, nb), (0, 4, 1, 2, 3))

    return (xp.reshape(n_seq, t_len, d), xs.reshape(nb, s_len, d), prompt_state(ktp), prompt_state(vtp),
            st["cp"], st["pp"], sample_state(kts), sample_state(vts), st["cs"], st["ps"], st["sg"])
```

```python
import functools

import jax
import jax.numpy as jnp
from jax import lax
from jax.experimental import pallas as pl
from jax.experimental.pallas import tpu as pltpu

F32 = jnp.float32
BF16 = jnp.bfloat16

NORM_EPS = 1e-6
CONV_TAPS = 3
POOL_WINDOWS = (2, 4, 8, 16)
POOL_PAD = max(POOL_WINDOWS) - 1
SG_CHUNK = 128
N_MOD = 6
N_BRANCH = 4

LANES = 128
SUBLANES = 8
KEY_BLOCK = 128
Q_TILE = 256
ATTN_HEAD_PAIRS = 1
VMEM_LIMIT_BYTES = 56 * 1024 * 1024


def _params(n_axes):
    return pltpu.CompilerParams(
        dimension_semantics=("arbitrary",) * n_axes, vmem_limit_bytes=VMEM_LIMIT_BYTES)


def _dot(a, b):
    return jnp.dot(a, b, preferred_element_type=F32)


def _dot_nt(a, b):
    return lax.dot_general(a, b, (((1,), (1,)), ((), ())), preferred_element_type=F32)


def _sigmoid(x):
    return 1.0 / (1.0 + jnp.exp(-x))


def _rms(x, g):
    return x * lax.rsqrt(jnp.mean(x * x, axis=-1, keepdims=True) + NORM_EPS) * g


def _norm_mod(x, g, sc, sh):
    return _rms(x, g) * (1.0 + sc) + sh


def _adaln_kernel(c_ref, w_ref, b_ref, os_ref, op_ref):
    c = c_ref[...]
    a = (c * _sigmoid(c)).astype(BF16)
    r = _dot(a, w_ref[...].astype(BF16)) + b_ref[...]
    ns = os_ref.shape[0]
    os_ref[...] = r[:ns]
    op_ref[...] = r[ns:]


def _adaln(c_all, w_ada, b_ada, n_sample, n_prompt_pad):
    depth, d, n = w_ada.shape
    tn = 1024
    rows = c_all.shape[0]
    return pl.pallas_call(
        _adaln_kernel,
        grid=(depth, n // tn),
        in_specs=[
            pl.BlockSpec((rows, d), lambda l, j: (0, 0)),
            pl.BlockSpec((None, d, tn), lambda l, j: (l, 0, j)),
            pl.BlockSpec((None, 1, tn), lambda l, j: (l, 0, j)),
        ],
        out_specs=[
            pl.BlockSpec((None, n_sample, tn), lambda l, j: (l, 0, j)),
            pl.BlockSpec((None, n_prompt_pad, tn), lambda l, j: (l, 0, j)),
        ],
        out_shape=[
            jax.ShapeDtypeStruct((depth, n_sample, n), F32),
            jax.ShapeDtypeStruct((depth, n_prompt_pad, n), F32),
        ],
        compiler_params=_params(2),
        name="adaln",
    )(c_all, w_ada, b_ada.reshape(depth, 1, n))


class _Group:
    def __init__(self, kind, n_tiles, tm, d, tiles_per_seq=None):
        self.kind, self.n_tiles, self.tm, self.d, self.tps = kind, n_tiles, tm, d, tiles_per_seq

    def x_spec(self, width, col_blocks=1, col=0):
        if self.kind == "prompt":
            return pl.BlockSpec((self.tm, width), lambda i, *_: (i, col))
        return pl.BlockSpec((self.tm, width), lambda i, *_: (0, i * col_blocks + col))

    def mod_spec(self, layer, chunk):
        if self.kind == "prompt":
            tps = self.tps
            return pl.BlockSpec((None, None, 1, self.d), lambda i, *_: (layer, i // tps, 0, chunk))
        return pl.BlockSpec((None, self.tm, self.d), lambda i, *_: (layer, 0, chunk))


def _const_spec(shape):
    nd = len(shape)
    return pl.BlockSpec(shape, lambda *_: (0,) * nd)


def _layer_spec(shape, layer):
    nd = len(shape)
    return pl.BlockSpec((None,) + tuple(shape), lambda *_: (layer,) + (0,) * nd)


def _in_proj_kernel(x_ref, sh_ref, sc_ref, g_ref, w_ref, *rest, q_scale):
    q_ref, k_ref, v_ref, kt_ref, vt_ref, r_ref = rest[-6:]
    h = _norm_mod(x_ref[...], g_ref[...], sc_ref[...], sh_ref[...]).astype(BF16)
    bw = q_ref.shape[1]
    p = _dot(h, w_ref[:, :3 * bw])
    k, v = p[:, bw:2 * bw], p[:, 2 * bw:3 * bw]
    q_ref[...] = (p[:, :bw] * q_scale).astype(q_ref.dtype)
    k_ref[...] = k.astype(k_ref.dtype)
    v_ref[...] = v.astype(v_ref.dtype)
    kt_ref[...] = k.T
    vt_ref[...] = v.T
    r_ref[...] = _dot(h, w_ref[:, 3 * bw:])


def _in_proj(grp, layer, x, mod, norm_w, w_in, bw, n_cols, q_scale, kv_dtype, state_shape, state_kt,
             state_vt):
    d = grp.d
    rows, cols = x.shape
    n_pos = cols // d
    n_rest = n_cols - 3 * bw
    o_spec = grp.x_spec(bw)
    if grp.kind == "prompt":
        tps = grp.tps
        t_spec = pl.BlockSpec((None, None, bw, grp.tm), lambda i: (layer, i // tps, 0, i % tps))
    else:
        t_spec = pl.BlockSpec((None, None, bw, grp.tm), lambda i: (layer, i, 0, 0))
    carried = [] if state_kt is None else [state_kt, state_vt]
    n_in = 5
    return pl.pallas_call(
        functools.partial(_in_proj_kernel, q_scale=q_scale),
        grid=(grp.n_tiles,),
        in_specs=[
            grp.x_spec(d),
            grp.mod_spec(layer, 0),
            grp.mod_spec(layer, 1),
            _layer_spec((1, d), layer),
            _layer_spec((d, n_cols), layer),
        ] + [pl.BlockSpec(memory_space=pl.ANY)] * len(carried),
        out_specs=[o_spec, o_spec, o_spec, t_spec, t_spec, grp.x_spec(n_rest)],
        out_shape=[
            jax.ShapeDtypeStruct((rows, n_pos * bw), BF16),
            jax.ShapeDtypeStruct((rows, n_pos * bw), kv_dtype),
            jax.ShapeDtypeStruct((rows, n_pos * bw), kv_dtype),
            jax.ShapeDtypeStruct(state_shape, F32),
            jax.ShapeDtypeStruct(state_shape, F32),
            jax.ShapeDtypeStruct((rows, n_pos * n_rest), F32),
        ],
        input_output_aliases={n_in: 3, n_in + 1: 4} if carried else {},
        compiler_params=_params(1),
        name="in_proj",
    )(x, mod, mod, norm_w, w_in, *carried)


def _suffix_prod_sublanes(x):
    sub = lax.broadcasted_iota(jnp.int32, x.shape, 0)
    y = x
    for k in (1, 2, 4):
        shifted = pltpu.roll(y, SUBLANES - k, axis=0)
        y = y * jnp.where(sub + k < SUBLANES, shifted, 1.0)
    return y


def _shift_up_sublanes(y):
    sub = lax.broadcasted_iota(jnp.int32, y.shape, 0)
    return jnp.where(sub + 1 < SUBLANES, pltpu.roll(y, SUBLANES - 1, axis=0), 1.0)


def _sb_tile_permuted(s_tile, carry, diag_offset):
    n = s_tile.shape[1]
    if n > LANES:
        parts = [_sb_tile_permuted(s_tile[:, c:c + LANES], carry[:, c:c + LANES],
                                   None if diag_offset is None else diag_offset - c)
                 for c in range(0, n, LANES)]
        return (jnp.concatenate([p[0] for p in parts], axis=1),
                jnp.concatenate([p[1] for p in parts], axis=1))
    if diag_offset is not None and diag_offset >= n:
        return jnp.zeros(s_tile.shape, BF16), carry
    rest = 0.5 - 0.5 * jnp.tanh(s_tile)
    if diag_offset is not None:
        r = lax.broadcasted_iota(jnp.int32, s_tile.shape, 0)
        q_idx = lax.broadcasted_iota(jnp.int32, s_tile.shape, 1)
        k_idx = (r & 7) * 16 + (r >> 3) + diag_offset
        rest = jnp.where(k_idx < q_idx, rest, 1.0)
    nv = KEY_BLOCK // SUBLANES
    taken = [None] * nv
    acc = None
    for v in range(nv - 1, -1, -1):
        nxt = rest[v * SUBLANES:(v + 1) * SUBLANES]
        nxt = nxt if acc is None else acc * nxt
        taken[v] = (1.0 - nxt) if acc is None else (acc - nxt)
        acc = nxt
    incl = _suffix_prod_sublanes(acc)
    base = _shift_up_sublanes(incl) * carry
    w = jnp.concatenate([t * base for t in taken], axis=0).astype(BF16)
    return w, carry * incl[0:1]


def _attn_prompt_kernel(bias_ref, q_ref, k_ref, v_ref, o_ref, kp_scr, vt_scr, qt_scr, s_a, s_b, w_a, w_b,
                        acc_scr, *, hd, n_pairs):
    hp = pl.program_id(1)
    i = pl.program_id(2)
    tq = q_ref.shape[0]
    n_chunks = k_ref.shape[0] // tq
    halves = tq // KEY_BLOCK

    @pl.when(i == 0)
    def _():
        r = lax.broadcasted_iota(jnp.int32, (KEY_BLOCK, KEY_BLOCK), 0)
        j = lax.broadcasted_iota(jnp.int32, (KEY_BLOCK, KEY_BLOCK), 1)
        perm = jnp.where(j == (r & 7) * 16 + (r >> 3), 1.0, 0.0).astype(BF16)
        ones0 = jnp.where((j >= hd) & (j < hd + 2), 1.0, 0.0)
        ones1 = jnp.where(j < 2, 1.0, 0.0)

        def body(c, carry):
            for hf in range(halves):
                rows = pl.ds(pl.multiple_of(c * tq + hf * KEY_BLOCK, KEY_BLOCK), KEY_BLOCK)
                dst = slice(hf * KEY_BLOCK, (hf + 1) * KEY_BLOCK)
                for p in range(n_pairs):
                    cols = slice(p * LANES, (p + 1) * LANES)
                    kvp = _dot(perm, jnp.concatenate([k_ref[rows, cols], v_ref[rows, cols]], axis=1))
                    kp, vp = kvp[:, :LANES], kvp[:, LANES:]
                    kp_scr[2 * p, c, dst, :] = jnp.where(j < hd, kp, ones0).astype(BF16)
                    kp_scr[2 * p + 1, c, dst, :] = jnp.where(j >= hd, kp, ones1).astype(BF16)
                    vt_scr[p, c, :, dst] = vp.T.astype(BF16)
            return carry

        lax.fori_loop(0, n_chunks, body, 0, unroll=4)

    lane = lax.broadcasted_iota(jnp.int32, (tq, LANES), 1)
    for p in range(n_pairs):
        q = q_ref[:, p * LANES:(p + 1) * LANES].astype(F32)
        for h in range(2):
            b = jnp.full(q.shape, 0.5 * bias_ref[2 * (hp * n_pairs + p) + h], F32)
            b_hi = b.astype(BF16).astype(F32)
            b_lo = b - b_hi
            own = (lane < hd) if h == 0 else (lane >= hd)
            first = hd if h == 0 else 0
            qm = jnp.where(own, q, jnp.where(lane == first, b_hi, jnp.where(lane == first + 1, b_lo, 0.0)))
            qt_scr[2 * p + h] = qm.T.astype(BF16)
    acc_scr[...] = jnp.zeros_like(acc_scr)

    n_heads = 2 * n_pairs

    def scores(k, s_buf):
        c = jnp.maximum(i - k, 0)
        for h in range(n_heads):
            s_buf[h] = _dot(kp_scr[h, c], qt_scr[h])

    def scan(s_buf, w_buf, diag, carry):
        out = []
        for h in range(n_heads):
            cr = carry[h]
            for hf in range(halves - 1, -1, -1):
                rows = slice(hf * KEY_BLOCK, (hf + 1) * KEY_BLOCK)
                w, cr = _sb_tile_permuted(s_buf[h, rows, :], cr,
                                          hf * KEY_BLOCK if diag else None)
                w_buf[h, rows, :] = w
            out.append(cr)
        return tuple(out)

    def values(k, w_buf):
        for h in range(n_heads):
            acc_scr[h] += _dot(vt_scr[h // 2, i - k], w_buf[h])

    n = i + 1
    ones = jnp.ones((1, tq), F32)
    scores(0, s_a)
    scores(1, s_b)
    carry = scan(s_a, w_a, True, (ones,) * n_heads)

    def pair(m, cr):
        k = 2 * m
        scores(k, s_a)
        cr = scan(s_b, w_b, False, cr)
        values(k - 2, w_a)
        scores(k + 1, s_b)
        cr = scan(s_a, w_a, False, cr)
        values(k - 1, w_b)
        return cr

    carry = lax.fori_loop(1, (n - 1) // 2 + 1, pair, carry)

    @pl.when(n % 2 == 0)
    def _():
        scan(s_b, w_b, False, carry)
        values(n - 2, w_a)
        values(n - 1, w_b)

    @pl.when(n % 2 == 1)
    def _():
        values(n - 1, w_a)

    row = lax.broadcasted_iota(jnp.int32, (2 * hd, tq), 0)
    for p in range(n_pairs):
        res = jnp.where(row < hd, acc_scr[2 * p], acc_scr[2 * p + 1])
        o_ref[:, p * LANES:(p + 1) * LANES] = res.T.astype(o_ref.dtype)


def _attn_prompt(q, k, v, bias, hd):
    b, t, c = q.shape
    n_pairs = ATTN_HEAD_PAIRS
    width = n_pairs * LANES
    assert 2 * hd == LANES and t % Q_TILE == 0 and c % width == 0
    n_chunks = t // Q_TILE
    nh = 2 * n_pairs
    qo_spec = pl.BlockSpec((None, Q_TILE, width), lambda bi, hp, i: (bi, i, hp))
    kv_spec = pl.BlockSpec((None, t, width), lambda bi, hp, i: (bi, 0, hp))
    return pl.pallas_call(
        functools.partial(_attn_prompt_kernel, hd=hd, n_pairs=n_pairs),
        grid=(b, c // width, n_chunks),
        in_specs=[pl.BlockSpec(memory_space=pltpu.SMEM), qo_spec, kv_spec, kv_spec],
        out_specs=qo_spec,
        out_shape=jax.ShapeDtypeStruct((b, t, c), BF16),
        scratch_shapes=[
            pltpu.VMEM((nh, n_chunks, Q_TILE, LANES), BF16),
            pltpu.VMEM((n_pairs, n_chunks, LANES, Q_TILE), BF16),
            pltpu.VMEM((nh, LANES, Q_TILE), BF16),
            pltpu.VMEM((nh, Q_TILE, Q_TILE), F32),
            pltpu.VMEM((nh, Q_TILE, Q_TILE), F32),
            pltpu.VMEM((nh, Q_TILE, Q_TILE), BF16),
            pltpu.VMEM((nh, Q_TILE, Q_TILE), BF16),
            pltpu.VMEM((nh, LANES, Q_TILE), F32),
        ],
        compiler_params=_params(3),
        name="attn_prompt",
    )(bias, q, k, v)


def _suffix_prod_lanes(x):
    n = x.shape[1]
    lane = lax.broadcasted_iota(jnp.int32, x.shape, 1)
    y = x
    k = 1
    while k < n:
        shifted = pltpu.roll(y, n - k, axis=1)
        y = y * jnp.where(lane + k < n, shifted, 1.0)
        k *= 2
    return y


def _attn_decode_kernel(pt_ref, bias_ref, q_ref, kn_ref, vn_ref, *rest, hd, n_pages):
    del pt_ref
    k_pages = rest[:n_pages]
    v_pages = rest[n_pages:2 * n_pages]
    o_ref = rest[2 * n_pages]
    s_len, c = q_ref.shape
    n_heads = c // hd
    nrow = s_len * n_heads

    head_row = lax.broadcasted_iota(jnp.int32, (n_heads, c), 0)
    lane = lax.broadcasted_iota(jnp.int32, (n_heads, c), 1)
    head_mask = (lane >= head_row * hd) & (lane < (head_row + 1) * hd)

    q = q_ref[...].astype(F32)
    q_rows = jnp.concatenate(
        [jnp.where(head_mask, jnp.broadcast_to(q[t:t + 1], (n_heads, c)), 0.0) for t in range(s_len)],
        axis=0).astype(BF16)
    bias_half = 0.5 * bias_ref[...]

    pad = jnp.zeros((KEY_BLOCK - s_len, c), F32)
    k_new = jnp.concatenate([kn_ref[...], pad], axis=0).astype(BF16)
    v_new = jnp.concatenate([vn_ref[...], pad], axis=0).astype(BF16)
    row = lax.broadcasted_iota(jnp.int32, (nrow, KEY_BLOCK), 0)
    key = lax.broadcasted_iota(jnp.int32, (nrow, KEY_BLOCK), 1)
    own_valid = key * n_heads + n_heads <= row

    pages = range(n_pages - 1, -1, -1)
    def rest_of(s):
        return 0.5 - 0.5 * jnp.tanh(s + bias_half)

    rest_all = jnp.concatenate(
        [jnp.where(own_valid, rest_of(_dot_nt(q_rows, k_new)), 1.0)]
        + [rest_of(_dot(q_rows, k_pages[p][...].reshape(c, KEY_BLOCK).astype(BF16))) for p in pages],
        axis=0)
    incl = _suffix_prod_lanes(rest_all)
    lane = lax.broadcasted_iota(jnp.int32, incl.shape, 1)
    later = jnp.where(lane + 1 < KEY_BLOCK, pltpu.roll(incl, KEY_BLOCK - 1, axis=1), 1.0)
    taken = later - incl
    total = jnp.broadcast_to(incl[:, 0:1], incl.shape)
    carry = None
    acc = None
    for j in range(n_pages + 1):
        rows = slice(j * nrow, (j + 1) * nrow)
        w = (taken[rows] if carry is None else taken[rows] * carry).astype(BF16)
        carry = total[rows] if carry is None else carry * total[rows]
        if j == 0:
            acc = _dot(w, v_new)
        else:
            acc = acc + _dot_nt(w, v_pages[n_pages - j][...].reshape(c, KEY_BLOCK).astype(BF16))

    outs = [jnp.sum(jnp.where(head_mask, acc[t * n_heads:(t + 1) * n_heads], 0.0), axis=0, keepdims=True)
            for t in range(s_len)]
    o_ref[...] = jnp.concatenate(outs, axis=0)


def _attn_decode(layer, q, k_new, v_new, cache_kt, cache_vt, page_table, bias_rows, hd):
    nb, s_len, c = q.shape
    n_pages = page_table.shape[1]
    n_heads, page = cache_kt.shape[2], cache_kt.shape[4]
    nrow = s_len * n_heads
    assert page == KEY_BLOCK and n_heads == SUBLANES and n_heads * hd == c
    seq_spec = pl.BlockSpec((None, s_len, c), lambda b, pt: (b, 0, 0))

    def page_spec(p):
        return pl.BlockSpec((None, None, n_heads, hd, page), lambda b, pt: (layer, pt[b, p], 0, 0, 0))

    grid_spec = pltpu.PrefetchScalarGridSpec(
        num_scalar_prefetch=1,
        grid=(nb,),
        in_specs=[pl.BlockSpec((None, nrow, KEY_BLOCK), lambda b, pt: (layer, 0, 0)),
                  seq_spec, seq_spec, seq_spec]
        + [page_spec(p) for p in range(n_pages)] + [page_spec(p) for p in range(n_pages)],
        out_specs=seq_spec,
    )
    return pl.pallas_call(
        functools.partial(_attn_decode_kernel, hd=hd, n_pages=n_pages),
        grid_spec=grid_spec,
        out_shape=jax.ShapeDtypeStruct((nb, s_len, c), F32),
        compiler_params=_params(1),
        name="attn_decode",
    )(page_table, bias_rows, q, k_new, v_new, *([cache_kt] * n_pages), *([cache_vt] * n_pages))


def _merge(x, ys, gates, wb_ref, wo_ref, g1):
    merged = None
    for i, (y, gate) in enumerate(zip(ys, gates)):
        term = _sigmoid(gate) * _dot(y.astype(BF16), wb_ref[i])
        merged = term if merged is None else merged + term
    return x + g1 * _dot(merged.astype(BF16), wo_ref[...])


def _pool_count(pos, win):
    return jnp.minimum(pos + 1.0, float(win))


def _branch_gates(x, nw_ref, sc_ref, sh_ref, wg_ref, d):
    h = _norm_mod(x, nw_ref[...], sc_ref[...], sh_ref[...]).astype(BF16)
    return [_dot(h, wg_ref[:, i * d:(i + 1) * d]) for i in range(N_BRANCH)]


def _mixer_prompt_kernel(x_ref, ya_ref, cv_ref, cvh_ref, pi_ref, pih_ref, su_ref, sv_ref,
                         sh_ref, sc_ref, nw_ref, wg_ref, cw_ref, pw_ref, ps_ref, sw_ref, sb_ref,
                         sn_ref, wb_ref, wo_ref, m1_ref, o_ref, cst_ref, pst_ref, cbuf, pbuf,
                         *, bw, tiles_per_seq):
    tm = x_ref.shape[0]
    first = (pl.program_id(0) % tiles_per_seq) == 0
    hc = cvh_ref.shape[0]
    hp = pih_ref.shape[0]

    cv = cv_ref[...]
    cx = cv[:, 2 * bw:3 * bw] * cv[:, 0:bw]
    cvh = cvh_ref[...]
    cbuf[0:hc, :] = jnp.where(first, 0.0, cvh[:, 2 * bw:3 * bw] * cvh[:, 0:bw])
    cbuf[hc:hc + tm, :] = cx
    cw = cw_ref[...]
    conv = cw[0:1] * cbuf[pl.ds(hc - 2, tm), :] + cw[1:2] * cbuf[pl.ds(hc - 1, tm), :] + cw[2:3] * cx
    y_b = cv[:, bw:2 * bw] * conv
    cst_ref[...] = cx[tm - hc:tm]

    pin = pi_ref[...]
    pbuf[0:hp, :] = jnp.where(first, 0.0, pih_ref[...])
    pbuf[hp:hp + tm, :] = pin
    pst_ref[...] = pin[tm - hp:tm]
    pos = (lax.broadcasted_iota(jnp.int32, (tm, 1), 0)
           + (pl.program_id(0) % tiles_per_seq) * tm).astype(F32)
    gw = bw // len(POOL_WINDOWS)
    y_c = []
    for g, win in enumerate(POOL_WINDOWS):
        lanes = slice(g * gw, (g + 1) * gw)
        wsum = pin[:, lanes]
        for kk in range(1, win):
            wsum = wsum + pbuf[pl.ds(hp - kk, tm), lanes]
        diff = wsum / _pool_count(pos, win) - pin[:, lanes]
        y_c.append(_dot(diff.astype(BF16), pw_ref[g]))
    y_c = jnp.concatenate(y_c, axis=1) * ps_ref[...]

    vn = _rms(sv_ref[...], sn_ref[...])
    n_groups = sw_ref.shape[0] // SG_CHUNK
    gwd = bw // n_groups
    r = lax.broadcasted_iota(jnp.int32, sw_ref.shape, 0) & (SG_CHUNK - 1)
    s_idx = lax.broadcasted_iota(jnp.int32, sw_ref.shape, 1)
    w_tril = jnp.where(s_idx <= r, sw_ref[...], 0.0).astype(BF16)
    lane_grp = lax.broadcasted_iota(jnp.int32, (SG_CHUNK, bw), 1)
    mixed = []
    for n in range(tm // SG_CHUNK):
        full = _dot(w_tril, vn[n * SG_CHUNK:(n + 1) * SG_CHUNK].astype(BF16))
        m = None
        for g in range(n_groups):
            blk = full[g * SG_CHUNK:(g + 1) * SG_CHUNK]
            sel = jnp.where((lane_grp >= g * gwd) & (lane_grp < (g + 1) * gwd), blk, 0.0)
            m = sel if m is None else m + sel
        mixed.append(m + sb_ref[...])
    y_d = su_ref[...] * jnp.concatenate(mixed, axis=0)

    x = x_ref[...]
    gates = _branch_gates(x, nw_ref, sc_ref, sh_ref, wg_ref, x.shape[1])
    o_ref[...] = _merge(x, (ya_ref[...], y_b, y_c, y_d), gates, wb_ref, wo_ref, m1_ref[...])


def _mixer_prompt(grp, layer, x, ya, rest, mod, norm_w, w_gate, conv_w, pool_w, pool_scale, sg_w, sg_bt,
                  sg_norm, w_branch, w_o, bw, n_seq):
    d, tm, tps = grp.d, grp.tm, grp.tps
    rows = x.shape[0]
    hc, hp = SUBLANES, 2 * SUBLANES
    assert hc >= CONV_TAPS - 1 and hp >= POOL_PAD and tm % SG_CHUNK == 0

    def halo(h, width, col):
        per = tm // h
        return pl.BlockSpec((h, width), lambda i: (jnp.maximum(i * per - 1, 0), col))

    n_groups = sg_w.shape[1]
    return pl.pallas_call(
        functools.partial(_mixer_prompt_kernel, bw=bw, tiles_per_seq=tps),
        grid=(grp.n_tiles,),
        in_specs=[
            grp.x_spec(d), grp.x_spec(bw),
            grp.x_spec(3 * bw, col=0), halo(hc, 3 * bw, 0),
            grp.x_spec(bw, col=3), halo(hp, bw, 3),
            grp.x_spec(bw, col=4), grp.x_spec(bw, col=5),
            grp.mod_spec(layer, 0), grp.mod_spec(layer, 1),
            _layer_spec((1, d), layer),
            _layer_spec(w_gate.shape[1:], layer),
            _layer_spec((CONV_TAPS, bw), layer),
            _layer_spec(pool_w.shape[1:], layer),
            _layer_spec((1, bw), layer),
            _layer_spec((n_groups * SG_CHUNK, SG_CHUNK), layer),
            _layer_spec((SG_CHUNK, bw), layer),
            _layer_spec((1, bw), layer),
            _layer_spec(w_branch.shape[1:], layer),
            _layer_spec(w_o.shape[1:], layer),
            grp.mod_spec(layer, 2),
        ],
        out_specs=[
            grp.x_spec(d),
            pl.BlockSpec((None, hc, bw), lambda i: (i // tps, 0, 0)),
            pl.BlockSpec((None, hp, bw), lambda i: (i // tps, 0, 0)),
        ],
        out_shape=[
            jax.ShapeDtypeStruct((rows, d), F32),
            jax.ShapeDtypeStruct((n_seq, hc, bw), F32),
            jax.ShapeDtypeStruct((n_seq, hp, bw), F32),
        ],
        scratch_shapes=[pltpu.VMEM((hc + tm, bw), F32), pltpu.VMEM((hp + tm, bw), F32)],
        compiler_params=_params(1),
        name="mixer_prompt",
    )(x, ya, rest, rest, rest, rest, rest, rest, mod, mod, norm_w, w_gate,
      conv_w, pool_w, pool_scale, sg_w.reshape(sg_w.shape[0], n_groups * SG_CHUNK, SG_CHUNK),
      sg_bt, sg_norm, w_branch, w_o, mod)


def _mixer_decode_kernel(x_ref, ya_ref, r_ref, cs_ref, ps_ref, sh_ref, sc_ref, nw_ref, wg_ref,
                         cw_ref, pw_ref, psc_ref, swv_ref, sbv_ref, sn_ref, wb_ref, wo_ref, m1_ref,
                         o_ref, cst_ref, pst_ref, sgv_ref, *, bw, d, s_len, pos0):
    n_rest = r_ref.shape[1] // s_len

    def rest(t, off, width):
        return r_ref[:, t * n_rest + off:t * n_rest + off + width]

    cw = cw_ref[...]
    cx = [rest(t, 2 * bw, bw) * rest(t, 0, bw) for t in range(s_len)]
    taps = CONV_TAPS - 1
    xp = [cs_ref[:, i * bw:(i + 1) * bw] for i in range(taps)] + cx
    y_b = []
    for t in range(s_len):
        conv = cw[0:1] * xp[t]
        for i in range(1, CONV_TAPS):
            conv = conv + cw[i:i + 1] * xp[t + i]
        y_b.append(rest(t, bw, bw) * conv)
    for i in range(taps):
        cst_ref[:, i * bw:(i + 1) * bw] = xp[len(xp) - taps + i]

    pin = [rest(t, 3 * bw, bw) for t in range(s_len)]
    pp = [ps_ref[:, i * bw:(i + 1) * bw] for i in range(POOL_PAD)] + pin
    for i in range(POOL_PAD):
        pst_ref[:, i * bw:(i + 1) * bw] = pp[len(pp) - POOL_PAD + i]
    gw = bw // len(POOL_WINDOWS)
    diffs = [[] for _ in POOL_WINDOWS]
    for t in range(s_len):
        for g, win in enumerate(POOL_WINDOWS):
            lanes = slice(g * gw, (g + 1) * gw)
            wsum = pp[POOL_PAD + t][:, lanes]
            for kk in range(1, win):
                wsum = wsum + pp[POOL_PAD + t - kk][:, lanes]
            cnt = float(min(pos0 + t + 1, win))
            diffs[g].append(wsum / cnt - pin[t][:, lanes])
    y_c_groups = [_dot(jnp.concatenate(diffs[g], axis=0).astype(BF16), pw_ref[g])
                  for g in range(len(POOL_WINDOWS))]
    y_c_all = jnp.concatenate(y_c_groups, axis=1) * psc_ref[...]

    tb = x_ref.shape[0]
    vn = [_rms(rest(t, 5 * bw, bw), sn_ref[...]) for t in range(s_len)]
    y_d = []
    for t in range(s_len):
        sgv_ref[:, t * bw:(t + 1) * bw] = vn[t]
        mixed = sbv_ref[t:t + 1]
        for s in range(t + 1):
            mixed = mixed + swv_ref[t * s_len + s:t * s_len + s + 1] * vn[s]
        y_d.append(rest(t, 4 * bw, bw) * mixed)

    cat = lambda parts: jnp.concatenate(parts, axis=0)
    x_all = cat([x_ref[:, t * d:(t + 1) * d] for t in range(s_len)])
    ya_all = cat([ya_ref[:, t * bw:(t + 1) * bw] for t in range(s_len)])
    rep = lambda ref: cat([ref[...]] * s_len)
    h_all = _norm_mod(x_all, nw_ref[...], rep(sc_ref), rep(sh_ref)).astype(BF16)
    gates = [_dot(h_all, wg_ref[:, i * d:(i + 1) * d]) for i in range(N_BRANCH)]
    g1 = rep(m1_ref)
    out = _merge(x_all, (ya_all, cat(y_b), y_c_all, cat(y_d)), gates, wb_ref, wo_ref, g1)
    for t in range(s_len):
        o_ref[:, t * d:(t + 1) * d] = out[t * tb:(t + 1) * tb]


def _mixer_decode(layer, x, ya, rest, state_conv, state_pool, mod, norm_w, w_gate, conv_w, pool_w,
                  pool_scale, sg_wv, sg_bv, sg_norm, w_branch, w_o, bw, d, s_len, pos0, tb):
    nb = x.shape[0]

    def row_spec(width):
        return pl.BlockSpec((tb, width), lambda i: (i, 0))

    def state_spec(width):
        return pl.BlockSpec((None, tb, width), lambda i: (layer, i, 0))

    def mod_spec(chunk):
        return pl.BlockSpec((None, tb, d), lambda i: (layer, i, chunk))

    return pl.pallas_call(
        functools.partial(_mixer_decode_kernel, bw=bw, d=d, s_len=s_len, pos0=pos0),
        grid=(nb // tb,),
        in_specs=[
            row_spec(s_len * d), row_spec(s_len * bw), row_spec(rest.shape[1]),
            state_spec(state_conv.shape[2]), state_spec(state_pool.shape[2]),
            mod_spec(0), mod_spec(1),
            _layer_spec((1, d), layer),
            _layer_spec(w_gate.shape[1:], layer),
            _layer_spec((CONV_TAPS, bw), layer),
            _layer_spec(pool_w.shape[1:], layer),
            _layer_spec((1, bw), layer),
            _layer_spec(sg_wv.shape[1:], layer),
            _layer_spec(sg_bv.shape[1:], layer),
            _layer_spec((1, bw), layer),
            _layer_spec(w_branch.shape[1:], layer),
            _layer_spec(w_o.shape[1:], layer),
            mod_spec(2),
        ],
        out_specs=[row_spec(s_len * d), row_spec(state_conv.shape[2]), row_spec(state_pool.shape[2]),
                   row_spec(s_len * bw)],
        out_shape=[
            jax.ShapeDtypeStruct((nb, s_len * d), F32),
            jax.ShapeDtypeStruct((nb, state_conv.shape[2]), F32),
            jax.ShapeDtypeStruct((nb, state_pool.shape[2]), F32),
            jax.ShapeDtypeStruct((nb, s_len * bw), F32),
        ],
        compiler_params=_params(1),
        name="mixer_decode",
    )(x, ya, rest, state_conv, state_pool, mod, mod, norm_w, w_gate, conv_w, pool_w, pool_scale,
      sg_wv, sg_bv, sg_norm, w_branch, w_o, mod)


def _ffn_kernel(x_ref, sh_ref, sc_ref, gt_ref, g_ref, w1_ref, w2_ref, fw_ref, o_ref, *, final_norm, tf):
    x = x_ref[...]
    h = _norm_mod(x, g_ref[...], sc_ref[...], sh_ref[...]).astype(BF16)
    acc = None
    for c in range(0, w1_ref.shape[1], tf):
        a = jnp.maximum(_dot(h, w1_ref[:, c:c + tf]), 0.0)
        part = _dot((a * a).astype(BF16), w2_ref[c:c + tf, :])
        acc = part if acc is None else acc + part
    y = x + gt_ref[...] * acc
    if final_norm:
        y = _rms(y, fw_ref[...])
    o_ref[...] = y


def _ffn(grp, layer, x, mod, norm_w, w1, w2, final_w, final_norm, tf):
    d = grp.d
    dff = w1.shape[-1]
    assert dff % tf == 0
    return pl.pallas_call(
        functools.partial(_ffn_kernel, final_norm=final_norm, tf=tf),
        grid=(grp.n_tiles,),
        in_specs=[
            grp.x_spec(d),
            grp.mod_spec(layer, 3), grp.mod_spec(layer, 4), grp.mod_spec(layer, 5),
            _layer_spec((1, d), layer),
            _layer_spec((d, dff), layer),
            _layer_spec((dff, d), layer),
            _const_spec((1, d)),
        ],
        out_specs=grp.x_spec(d),
        out_shape=jax.ShapeDtypeStruct(x.shape, F32),
        compiler_params=_params(1),
        name="ffn",
    )(x, mod, mod, mod, norm_w, w1, w2, final_w)


def _row_tile(t, want):
    tm = min(t, want)
    assert t % tm == 0
    return tm


def kernel(x_prompt, x_sample, cache_k, cache_v, state_conv, state_pool, page_table, c_prompt,
           c_sample, w_ada, b_ada, norm_mix, norm_ffn, w_in, sb_bias, conv_w, pool_w, pool_scale,
           sg_w, sg_b, sg_norm, w_branch, w_o, w_ff1, w_ff2, norm_final):
    n_seq, t_len, d = x_prompt.shape
    nb, s_len, _ = x_sample.shape
    depth = w_ada.shape[0]
    n_heads, hd = cache_k.shape[3], cache_k.shape[4]
    bw = n_heads * hd
    n_pages, page = page_table.shape[1], cache_k.shape[2]
    past_len = n_pages * page
    n_groups = sg_w.shape[1]
    gwd = bw // n_groups
    q_scale = 0.5 * hd ** -0.5

    w_in_b = w_in.astype(BF16)
    n_proj = 9 * bw
    w_gate = w_in_b[:, :, n_proj:]
    w_branch_b, w_o_b = w_branch.astype(BF16), w_o.astype(BF16)
    w_ff1_b, w_ff2_b = w_ff1.astype(BF16), w_ff2.astype(BF16)
    pool_w_b = pool_w.astype(BF16)
    norm_mix3, norm_ffn3 = norm_mix[:, None, :], norm_ffn[:, None, :]
    pool_scale3, sg_norm3 = pool_scale[:, None, :], sg_norm[:, None, :]
    norm_final2 = norm_final[None, :]
    chunk = min(t_len, SG_CHUNK)
    assert chunk == SG_CHUNK and sg_w.shape[2] == SG_CHUNK
    sg_bt = jnp.repeat(jnp.swapaxes(sg_b, 1, 2), gwd, axis=2)
    ls = min(s_len, SG_CHUNK)
    assert ls == s_len
    sg_wv = jnp.repeat(jnp.transpose(sg_w[:, :, :ls, :ls], (0, 2, 3, 1)), gwd, axis=3)
    sg_wv = sg_wv.reshape(depth, ls * ls, bw)
    sg_bv = jnp.repeat(jnp.swapaxes(sg_b[:, :, :ls], 1, 2), gwd, axis=2)
    bias_rows = jnp.broadcast_to(jnp.tile(sb_bias, (1, s_len))[:, :, None],
                                 (depth, s_len * n_heads, KEY_BLOCK))

    bp = -(-n_seq // SUBLANES) * SUBLANES
    c_all = jnp.concatenate([c_sample, c_prompt, jnp.zeros((bp - n_seq, d), F32)], axis=0)
    mod_s, mod_p = _adaln(c_all, w_ada, b_ada, nb, bp)
    mod_p = mod_p.reshape(depth, bp, 1, N_MOD * d)

    xp = x_prompt.reshape(n_seq * t_len, d)
    xs = x_sample.reshape(nb, s_len * d)
    ck = jnp.transpose(cache_k, (0, 1, 3, 4, 2))
    cv = jnp.transpose(cache_v, (0, 1, 3, 4, 2))
    sconv = state_conv.reshape(depth, nb, -1)
    spool = state_pool.reshape(depth, nb, -1)

    def prompt_group(want):
        tm = _row_tile(t_len, want)
        return _Group("prompt", n_seq * t_len // tm, tm, d, t_len // tm)

    sample_group = _Group("sample", s_len, nb, d)
    tf = 1024

    outs ={k: [] for k in ("cp", "pp", "cs", "ps", "sg")}
    ktp = vtp = kts = vts = None
    for l in range(depth):
        last = l == depth - 1
        q, k, v, ktp, vtp, rest = _in_proj(prompt_group(512), l, xp, mod_p, norm_mix3, w_in_b, bw, n_proj,
                                           q_scale, BF16, (depth, n_seq, bw, t_len), ktp, vtp)
        ya = _attn_prompt(q.reshape(n_seq, t_len, bw), k.reshape(n_seq, t_len, bw),
                          v.reshape(n_seq, t_len, bw), sb_bias[l], hd)
        gm = prompt_group(256)
        xp, cst, pst = _mixer_prompt(gm, l, xp, ya.reshape(n_seq * t_len, bw), rest, mod_p, norm_mix3,
                                     w_gate, conv_w, pool_w_b, pool_scale3, sg_w, sg_bt, sg_norm3,
                                     w_branch_b, w_o_b, bw, n_seq)
        xp = _ffn(prompt_group(512), l, xp, mod_p, norm_ffn3, w_ff1_b, w_ff2_b, norm_final2, last, tf)
        outs["cp"].append(cst[:, cst.shape[1] - (CONV_TAPS - 1):])
        outs["pp"].append(pst[:, pst.shape[1] - POOL_PAD:])

        q, k, v, kts, vts, rest = _in_proj(sample_group, l, xs, mod_s, norm_mix3, w_in_b, bw, n_proj,
                                           q_scale, F32, (depth, s_len, bw, nb), kts, vts)
        ya = _attn_decode(l, q.reshape(nb, s_len, bw), k.reshape(nb, s_len, bw),
                          v.reshape(nb, s_len, bw), ck, cv, page_table, bias_rows, hd)
        xs, cst, pst, sgv = _mixer_decode(l, xs, ya.reshape(nb, s_len * bw), rest, sconv, spool,
                                          mod_s, norm_mix3, w_gate, conv_w, pool_w_b, pool_scale3,
                                          sg_wv, sg_bv, sg_norm3, w_branch_b, w_o_b, bw, d, s_len,
                                          past_len, min(nb, 32))
        xs = _ffn(sample_group, l, xs, mod_s, norm_ffn3, w_ff1_b, w_ff2_b, norm_final2, last, tf)
        outs["cs"].append(cst.reshape(nb, CONV_TAPS - 1, bw))
        outs["ps"].append(pst.reshape(nb, POOL_PAD, bw))
        outs["sg"].append(sgv.reshape(nb, s_len, bw))

    st = {k: jnp.stack(v) for k, v in outs.items()}

    def prompt_state(a):
        return jnp.transpose(a.reshape(depth, n_seq, n_heads, hd, t_len), (0, 1, 4, 2, 3))

    def sample_state(a):
        return jnp.transpose(a.reshape(depth, s_len, n_heads, hd, nb), (0, 4, 1, 2, 3))

    return (xp.reshape(n_seq, t_len, d), xs.reshape(nb, s_len, d), prompt_state(ktp), prompt_state(vtp),
            st["cp"], st["pp"], sample_state(kts), sample_state(vts), st["cs"], st["ps"], st["sg"])
```
